```python
import math
import jax
import jax.numpy as jnp
from jax import lax
import numpy as np

D_MODEL = 4096
BATCH = 4
SEQ = 2048
DEPTH = 1

HEAD_DIM = 128
D_MIX = D_MODEL
ATTN_WIDTH = D_MIX // 2
N_ATTN_HEADS = ATTN_WIDTH // HEAD_DIM
POOL_WIDTH = D_MIX - ATTN_WIDTH
POOL_WINDOWS = (2, 4, 8, 16)
N_POOL_GROUPS = len(POOL_WINDOWS)
POOL_GROUP_DIM = POOL_WIDTH // N_POOL_GROUPS
DILATED_BRANCHES = ((128, 1), (512, 4), (2048, 16))
ATTN_BLOCK = 128
ROPE_THETA = 10000.0
IN_PROJ_WIDTH = 3 * ATTN_WIDTH + POOL_WIDTH
N_EXPERTS = 64
TOP_K = 8
N_EXPERT_GROUPS = 8
TOPK_GROUPS = 4
D_EXPERT = 512
D_SHARED = 512
ROUTED_SCALE = 2.5
MOE_BLOCK = 128
N_MOD = 6
NORM_EPS = 1e-6

kernel_name = 'hybrid_dilated_attn_pool_moe_block'


def rms_norm(x, g):
    xf = x.astype(jnp.float32)
    y = xf * lax.rsqrt(jnp.mean(xf * xf, axis=-1, keepdims=True) + NORM_EPS)
    return (y * g.astype(jnp.float32)).astype(x.dtype)


def rope(x, pos):
    hd = x.shape[-1]
    half = hd // 2
    inv_freq = jnp.exp(-math.log(ROPE_THETA) * jnp.arange(half, dtype=jnp.float32) / half)
    ang = pos.astype(jnp.float32)[:, None] * inv_freq[None, :]
    cos = jnp.cos(ang)[None, :, None, :]
    sin = jnp.sin(ang)[None, :, None, :]
    xf = x.astype(jnp.float32)
    x1, x2 = xf[..., :half], xf[..., half:]
    return jnp.concatenate([x1 * cos - x2 * sin, x2 * cos + x1 * sin], axis=-1).astype(x.dtype)


def dilated_branch(q, k, v, window, dilation):
    b, s, h, hd = q.shape
    steps = window // dilation
    qb = ATTN_BLOCK
    span = dilation * qb
    s_pad = -(-s // span) * span
    sub_len = s_pad // dilation
    nb = sub_len // qb

    def to_sub(t):
        t = jnp.pad(t, ((0, 0), (0, s_pad - s), (0, 0), (0, 0)))
        t = t.reshape(b, sub_len, dilation, h, hd).transpose(0, 2, 3, 1, 4)
        return t.reshape(b, dilation, h, nb, qb, hd)

    def with_prev(t):
        prev = jnp.pad(t, ((0, 0), (0, 0), (0, 0), (1, 0), (0, 0), (0, 0)))[:, :, :, :nb]
        return jnp.concatenate([prev, t], axis=4)

    qs = to_sub(q)
    kb = with_prev(to_sub(k))
    vb = with_prev(to_sub(v))
    scores = jnp.einsum('brhnqc,brhnkc->brhnqk', qs, kb,
                        preferred_element_type=jnp.float32) * (hd ** -0.5)
    qi = jnp.arange(qb)[:, None]
    ki = jnp.arange(2 * qb)[None, :]
    dist = qb + qi - ki
    band = (dist >= 0) & (dist <= steps)
    blk = jnp.arange(nb)[:, None, None]
    mask = band[None] & ((blk > 0) | (ki >= qb)[None])
    scores = jnp.where(mask, scores, -jnp.inf)
    m = jnp.max(scores, axis=-1, keepdims=True)
    p = jnp.exp(scores - m)
    l = jnp.sum(p, axis=-1)
    o = jnp.einsum('brhnqk,brhnkc->brhnqc', p, vb.astype(jnp.float32)) / l[..., None]
    lse = m[..., 0] + jnp.log(l)
    o = o.reshape(b, dilation, h, sub_len, hd).transpose(0, 3, 1, 2, 4).reshape(b, s_pad, h, hd)[:, :s]
    lse = lse.reshape(b, dilation, h, sub_len).transpose(0, 3, 1, 2).reshape(b, s_pad, h)[:, :s]
    return o, lse


def dilated_attention(q, k, v):
    outs, lses = [], []
    for window, dilation in DILATED_BRANCHES:
        o, lse = dilated_branch(q, k, v, window, dilation)
        outs.append(o)
        lses.append(lse)
    w = jax.nn.softmax(jnp.stack(lses, axis=0), axis=0)
    o = jnp.sum(w[..., None] * jnp.stack(outs, axis=0), axis=0)
    return o.astype(q.dtype)


def pool_mixer(u, w_pool, pool_scale):
    b, s, _ = u.shape
    ug = u.astype(jnp.float32).reshape(b, s, N_POOL_GROUPS, POOL_GROUP_DIM)
    cs = jnp.cumsum(ug, axis=1)
    pos = jnp.arange(s)
    mixed = []
    for g, w in enumerate(POOL_WINDOWS):
        c_g = cs[:, :, g]
        lag = jnp.pad(c_g, ((0, 0), (w, 0), (0, 0)))[:, :s]
        cnt = jnp.minimum(pos + 1, w).astype(jnp.float32)[None, :, None]
        mixed.append((c_g - lag) / cnt - ug[:, :, g])
    mix = jnp.stack(mixed, axis=2)
    y = jnp.einsum('bsgc,gce->bsge', mix, w_pool.astype(jnp.float32)) * pool_scale.astype(jnp.float32)
    return y.reshape(b, s, POOL_WIDTH).astype(u.dtype)


def route(h, w_router, router_bias):
    t = h.shape[0]
    scores = jax.nn.sigmoid(h.astype(jnp.float32) @ w_router.astype(jnp.float32))
    biased = scores + router_bias.astype(jnp.float32)
    grp = biased.reshape(t, N_EXPERT_GROUPS, N_EXPERTS // N_EXPERT_GROUPS)
    grp_score = jnp.sum(lax.top_k(grp, 2)[0], axis=-1)
    _, gidx = lax.top_k(grp_score, TOPK_GROUPS)
    gmask = jnp.sum(jax.nn.one_hot(gidx, N_EXPERT_GROUPS, dtype=jnp.float32), axis=1) > 0
    emask = jnp.repeat(gmask, N_EXPERTS // N_EXPERT_GROUPS, axis=1)
    _, idx = lax.top_k(jnp.where(emask, biased, -jnp.inf), TOP_K)
    wts = jnp.take_along_axis(scores, idx, axis=1)
    wts = wts / jnp.sum(wts, axis=-1, keepdims=True) * ROUTED_SCALE
    return idx.astype(jnp.int32), wts


def routed_experts(h, idx, wts, w_gate, w_up, w_down):
    t, d = h.shape
    n = t * TOP_K
    flat_e = idx.reshape(n)
    flat_tok = jnp.arange(n, dtype=jnp.int32) // TOP_K
    flat_w = wts.reshape(n)
    order = jnp.argsort(flat_e)
    se = flat_e[order]
    counts = jnp.bincount(flat_e, length=N_EXPERTS)
    starts = jnp.cumsum(counts) - counts
    padded = (counts + MOE_BLOCK - 1) // MOE_BLOCK * MOE_BLOCK
    pad_ends = jnp.cumsum(padded)
    pad_starts = pad_ends - padded
    dest = pad_starts[se] + jnp.arange(n, dtype=jnp.int32) - starts[se]
    n_blocks = -(-n // MOE_BLOCK) + N_EXPERTS
    rows = n_blocks * MOE_BLOCK
    row_tok = jnp.zeros((rows,), jnp.int32).at[dest].set(flat_tok[order])
    row_w = jnp.zeros((rows,), jnp.float32).at[dest].set(flat_w[order])
    block_e = jnp.minimum(
        jnp.searchsorted(pad_ends, jnp.arange(n_blocks) * MOE_BLOCK, side='right'), N_EXPERTS - 1)

    def body(y, blk):
        e, tok, wt = blk
        xb = h[tok]
        act = jax.nn.silu(xb @ w_gate[e]) * (xb @ w_up[e])
        ob = (act @ w_down[e]).astype(jnp.float32) * wt[:, None]
        return y.at[tok].add(ob), None

    y0 = jnp.zeros((t, d), jnp.float32)
    y, _ = lax.scan(body, y0, (block_e, row_tok.reshape(n_blocks, MOE_BLOCK),
                               row_w.reshape(n_blocks, MOE_BLOCK)))
    return y


def setup_inputs(seed: int = 0) -> dict:
    key = jax.random.key(seed)
    ks = jax.random.split(key, 24)
    f32 = jnp.float32

    def nrm(k, shape, scale):
        return jax.random.normal(k, shape, f32) * scale

    def gain(k, shape):
        return 1.0 + 0.1 * jax.random.normal(k, shape, f32)

    L = DEPTH
    return {
        'x': nrm(ks[0], (BATCH, SEQ, D_MODEL), 1.0),
        'c': nrm(ks[1], (BATCH, D_MODEL), 1.0),
        'w_mod': nrm(ks[2], (L, D_MODEL, N_MOD * D_MODEL), 0.5 * D_MODEL ** -0.5),
        'b_mod': nrm(ks[3], (L, N_MOD * D_MODEL), 0.02),
        'g_pre_mix': gain(ks[4], (L, D_MODEL)),
        'g_post_mix': gain(ks[5], (L, D_MODEL)),
        'g_pre_ffn': gain(ks[6], (L, D_MODEL)),
        'g_post_ffn': gain(ks[7], (L, D_MODEL)),
        'w_in': nrm(ks[8], (L, D_MODEL, IN_PROJ_WIDTH), D_MODEL ** -0.5),
        'w_pool': nrm(ks[9], (L, N_POOL_GROUPS, POOL_GROUP_DIM, POOL_GROUP_DIM), POOL_GROUP_DIM ** -0.5),
        'pool_scale': gain(ks[10], (L, N_POOL_GROUPS, POOL_GROUP_DIM)),
        'g_attn_out': gain(ks[11], (L, ATTN_WIDTH)),
        'g_pool_out': gain(ks[12], (L, POOL_WIDTH)),
        'w_out': nrm(ks[13], (L, D_MIX, D_MODEL), D_MIX ** -0.5),
        'w_router': nrm(ks[14], (L, D_MODEL, N_EXPERTS), D_MODEL ** -0.5),
        'router_bias': nrm(ks[15], (L, N_EXPERTS), 0.01),
        'w_gate': nrm(ks[16], (L, N_EXPERTS, D_MODEL, D_EXPERT), D_MODEL ** -0.5),
        'w_up': nrm(ks[17], (L, N_EXPERTS, D_MODEL, D_EXPERT), D_MODEL ** -0.5),
        'w_down': nrm(ks[18], (L, N_EXPERTS, D_EXPERT, D_MODEL), D_EXPERT ** -0.5),
        'w_shared_gate': nrm(ks[19], (L, D_MODEL, D_SHARED), D_MODEL ** -0.5),
        'w_shared_up': nrm(ks[20], (L, D_MODEL, D_SHARED), D_MODEL ** -0.5),
        'w_shared_down': nrm(ks[21], (L, D_SHARED, D_MODEL), D_SHARED ** -0.5),
    }


def reference(x, c, w_mod, b_mod, g_pre_mix, g_post_mix, g_pre_ffn, g_post_ffn,
              w_in, w_pool, pool_scale, g_attn_out, g_pool_out, w_out,
              w_router, router_bias, w_gate, w_up, w_down,
              w_shared_gate, w_shared_up, w_shared_down):
    b, s, d = x.shape
    dt = x.dtype
    pos = jnp.arange(s)
    cond = jax.nn.silu(c.astype(jnp.float32))
    h = x
    for layer in range(DEPTH):
        mod = (cond @ w_mod[layer].astype(jnp.float32) + b_mod[layer].astype(jnp.float32))
        mod = mod.reshape(b, N_MOD, d)
        shift1, scale1, gate1 = mod[:, 0, None], mod[:, 1, None], mod[:, 2, None]
        shift2, scale2, gate2 = mod[:, 3, None], mod[:, 4, None], mod[:, 5, None]

        a = (rms_norm(h, g_pre_mix[layer]) * (1.0 + scale1) + shift1).astype(dt)
        proj = a @ w_in[layer]
        q = proj[..., :ATTN_WIDTH].reshape(b, s, N_ATTN_HEADS, HEAD_DIM)
        k = proj[..., ATTN_WIDTH:2 * ATTN_WIDTH].reshape(b, s, N_ATTN_HEADS, HEAD_DIM)
        v = proj[..., 2 * ATTN_WIDTH:3 * ATTN_WIDTH].reshape(b, s, N_ATTN_HEADS, HEAD_DIM)
        u = proj[..., 3 * ATTN_WIDTH:]
        q = rope(q, pos)
        k = rope(k, pos)
        attn = dilated_attention(q, k, v).reshape(b, s, ATTN_WIDTH)
        pool = pool_mixer(u, w_pool[layer], pool_scale[layer])
        mixed = jnp.concatenate([rms_norm(attn, g_attn_out[layer]),
                                 rms_norm(pool, g_pool_out[layer])], axis=-1)
        y = mixed @ w_out[layer]
        h = (h + gate1 * rms_norm(y, g_post_mix[layer])).astype(dt)

        f = (rms_norm(h, g_pre_ffn[layer]) * (1.0 + scale2) + shift2).astype(dt)
        ft = f.reshape(b * s, d)
        idx, wts = route(ft, w_router[layer], router_bias[layer])
        routed = routed_experts(ft, idx, wts, w_gate[layer], w_up[layer], w_down[layer])
        shared = (jax.nn.silu(ft @ w_shared_gate[layer]) * (ft @ w_shared_up[layer])) @ w_shared_down[layer]
        moe = (routed + shared.astype(jnp.float32)).astype(dt).reshape(b, s, d)
        h = (h + gate2 * rms_norm(moe, g_post_ffn[layer])).astype(dt)
    return h
```

```python
import functools
import math

import jax
import jax.numpy as jnp
from jax import lax
from jax.experimental import pallas as pl
from jax.experimental.pallas import tpu as pltpu

F32 = jnp.float32
BF16 = jnp.bfloat16

HEAD_DIM = 128
ATTN_BLOCK = 128
DILATIONS = (1, 4, 16)
POOL_WINDOWS = (2, 4, 8, 16)
POOL_HALO = 16
ROPE_THETA = 10000.0
TOP_K = 8
N_EXPERT_GROUPS = 8
TOPK_GROUPS = 4
ROUTED_SCALE = 2.5
N_MOD = 6
NORM_EPS = 1e-6
MASK_VALUE = -1e30

LANES = 128
SUBLANES = 8
VMEM_LIMIT_BYTES = 60 * 1024 * 1024

MOE_ROWS = 128

_NT = (((1,), (1,)), ((), ()))


def _params(*sem):
    return pltpu.CompilerParams(dimension_semantics=sem, vmem_limit_bytes=VMEM_LIMIT_BYTES)


def _rms(x, g):
    return x * lax.rsqrt(jnp.mean(x * x, axis=-1, keepdims=True) + NORM_EPS) * g


def _silu(x):
    return x * jax.nn.sigmoid(x)


def _mod_kernel(c_ref, w_ref, b_ref, o_ref):
    cond = _silu(c_ref[...])
    o_ref[...] = jnp.dot(cond.astype(BF16), w_ref[...].astype(BF16),
                         preferred_element_type=F32) + b_ref[...]


def _modulation(c, w_mod, b_mod):
    b, d = c.shape
    n = w_mod.shape[1]
    bp = -(-b // SUBLANES) * SUBLANES
    tn = 512
    out = pl.pallas_call(
        _mod_kernel,
        grid=(n // tn,),
        in_specs=[pl.BlockSpec((bp, d), lambda j: (0, 0)),
                  pl.BlockSpec((d, tn), lambda j: (0, j)),
                  pl.BlockSpec((1, tn), lambda j: (0, j))],
        out_specs=pl.BlockSpec((bp, tn), lambda j: (0, j)),
        out_shape=jax.ShapeDtypeStruct((bp, n), F32),
        compiler_params=_params("arbitrary"),
        name="modulation",
    )(jnp.pad(c, ((0, bp - b), (0, 0))), w_mod, b_mod.reshape(1, n))
    return out[:b].reshape(b, N_MOD, d)


def _inproj_kernel(x_ref, g_ref, mod_ref, w_ref, cos_ref, sin_ref, o_ref, a_scr, *, n_rope_tiles):
    j = pl.program_id(1)

    @pl.when(j == 0)
    def _():
        shift1 = mod_ref[0, 0:1, :]
        scale1 = mod_ref[0, 1:2, :]
        a = _rms(x_ref[...], g_ref[...]) * (1.0 + scale1) + shift1
        a_scr[...] = a.astype(BF16)

    acc = jnp.dot(a_scr[...], w_ref[...], preferred_element_type=F32)

    @pl.when(j < n_rope_tiles)
    def _():
        cosf = cos_ref[...]
        sinf = sin_ref[...]
        for hh in range(acc.shape[1] // HEAD_DIM):
            blk = acc[:, hh * HEAD_DIM:(hh + 1) * HEAD_DIM]
            rot = blk * cosf + pltpu.roll(blk, HEAD_DIM // 2, 1) * sinf
            o_ref[:, hh * HEAD_DIM:(hh + 1) * HEAD_DIM] = rot.astype(o_ref.dtype)

    @pl.when(j >= n_rope_tiles)
    def _():
        o_ref[...] = acc.astype(o_ref.dtype)


def _in_projection(x2, g, mod, w_in_bf, cosf, sinf, seq, rope_width):
    t, d = x2.shape
    n = w_in_bf.shape[1]
    tm = min(512, seq)
    tn = min(1024, rope_width)
    tiles_per_seq = seq // tm
    return pl.pallas_call(
        functools.partial(_inproj_kernel, n_rope_tiles=rope_width // tn),
        grid=(t // tm, n // tn),
        in_specs=[pl.BlockSpec((tm, d), lambda i, j: (i, 0)),
                  pl.BlockSpec((1, d), lambda i, j: (0, 0)),
                  pl.BlockSpec((1, N_MOD, d), lambda i, j: (i // tiles_per_seq, 0, 0)),
                  pl.BlockSpec((d, tn), lambda i, j: (0, j)),
                  pl.BlockSpec((tm, HEAD_DIM), lambda i, j: (i % tiles_per_seq, 0)),
                  pl.BlockSpec((tm, HEAD_DIM), lambda i, j: (i % tiles_per_seq, 0))],
        out_specs=pl.BlockSpec((tm, tn), lambda i, j: (i, j)),
        out_shape=jax.ShapeDtypeStruct((t, n), BF16),
        scratch_shapes=[pltpu.VMEM((tm, d), BF16)],
        compiler_params=_params("arbitrary", "arbitrary"),
        name="in_projection",
    )(x2, g.reshape(1, d), mod, w_in_bf, cosf, sinf)


def _attn_kernel(q_ref, k_ref, v_ref, o_ref, qf, kf, vf, qs, ks, vs, accs, mls, *, seq):
    blk = ATTN_BLOCK
    scale = HEAD_DIM ** -0.5
    qf[...] = q_ref[...].astype(F32)
    kf[...] = k_ref[...].astype(F32)
    vf[...] = v_ref[...].astype(F32)
    ks[0:blk, :] = jnp.zeros((blk, HEAD_DIM), BF16)
    vs[0:blk, :] = jnp.zeros((blk, HEAD_DIM), BF16)

    row = lax.broadcasted_iota(jnp.int32, (blk, 2 * blk), 0)
    col = lax.broadcasted_iota(jnp.int32, (blk, 2 * blk), 1)
    in_band = ((col < blk) & (col >= row)) | ((col >= blk) & (col - blk <= row))
    band_bias = jnp.where(in_band, 0.0, MASK_VALUE)
    causal_bias = jnp.where(lax.broadcasted_iota(jnp.int32, (blk, blk), 1)
                            <= lax.broadcasted_iota(jnp.int32, (blk, blk), 0), 0.0, MASK_VALUE)
    prev_bias = jnp.where(col < blk, MASK_VALUE, 0.0)
    lane = lax.broadcasted_iota(jnp.int32, (blk, HEAD_DIM), 1)

    for bi, dil in enumerate(DILATIONS):
        sub_len = seq // dil
        n_blocks = sub_len // blk

        def sub_body(res, carry, dil=dil, sub_len=sub_len, n_blocks=n_blocks, bi=bi):
            if dil == 1:
                qs[0:sub_len, :] = q_ref[...]
                ks[blk:blk + sub_len, :] = k_ref[...]
                vs[blk:blk + sub_len, :] = v_ref[...]
            else:
                qs[0:sub_len, :] = qf[pl.ds(res, sub_len, stride=dil), :].astype(BF16)
                ks[blk:blk + sub_len, :] = kf[pl.ds(res, sub_len, stride=dil), :].astype(BF16)
                vs[blk:blk + sub_len, :] = vf[pl.ds(res, sub_len, stride=dil), :].astype(BF16)

            def blk_body(n, c2):
                r0 = pl.multiple_of(n * blk, blk)
                qb = qs[pl.ds(r0, blk), :]
                if n_blocks > 1:
                    kk = ks[pl.ds(r0, 2 * blk), :]
                    vv = vs[pl.ds(r0, 2 * blk), :]
                    bias = band_bias + jnp.where(n == 0, prev_bias, 0.0)
                else:
                    kk = ks[blk:2 * blk, :]
                    vv = vs[blk:2 * blk, :]
                    bias = causal_bias
                s = lax.dot_general(qb, kk, _NT, preferred_element_type=F32) * scale + bias
                m = jnp.max(s, axis=1, keepdims=True)
                p = jnp.exp(s - m)
                l = jnp.sum(p, axis=1, keepdims=True)
                acc = jnp.dot(p.astype(BF16), vv, preferred_element_type=F32)
                ml = jnp.where(lane == 0, m, l)
                if dil == 1:
                    accs[bi, pl.ds(r0, blk), :] = acc
                    mls[bi, pl.ds(r0, blk), :] = ml
                else:
                    start = res + dil * r0
                    accs[bi, pl.ds(start, blk, stride=dil), :] = acc
                    mls[bi, pl.ds(start, blk, stride=dil), :] = ml
                return c2

            lax.fori_loop(0, n_blocks, blk_body, 0)
            return carry

        lax.fori_loop(0, dil, sub_body, 0)

    def merge(c, carry):
        r0 = pl.multiple_of(c * blk, blk)
        stats = [mls[bi, pl.ds(r0, blk), :] for bi in range(len(DILATIONS))]
        ms = [st[:, 0:1] for st in stats]
        ls = [st[:, 1:2] for st in stats]
        m_all = functools.reduce(jnp.maximum, ms)
        ws = [jnp.exp(m - m_all) for m in ms]
        den = sum(w * l for w, l in zip(ws, ls))
        num = sum(w * accs[bi, pl.ds(r0, blk), :] for bi, w in enumerate(ws))
        o_ref[pl.ds(r0, blk), :] = (num * (1.0 / den)).astype(o_ref.dtype)
        return carry

    lax.fori_loop(0, seq // blk, merge, 0)


def _dilated_attention(proj, batch, seq, n_heads):
    t = proj.shape[0]
    nb = len(DILATIONS)
    return pl.pallas_call(
        functools.partial(_attn_kernel, seq=seq),
        grid=(batch, n_heads),
        in_specs=[pl.BlockSpec((seq, HEAD_DIM), lambda b, h: (b, h)),
                  pl.BlockSpec((seq, HEAD_DIM), lambda b, h: (b, n_heads + h)),
                  pl.BlockSpec((seq, HEAD_DIM), lambda b, h: (b, 2 * n_heads + h))],
        out_specs=pl.BlockSpec((seq, HEAD_DIM), lambda b, h: (b, h)),
        out_shape=jax.ShapeDtypeStruct((t, n_heads * HEAD_DIM), BF16),
        scratch_shapes=[pltpu.VMEM((seq, HEAD_DIM), F32),
                        pltpu.VMEM((seq, HEAD_DIM), F32),
                        pltpu.VMEM((seq, HEAD_DIM), F32),
                        pltpu.VMEM((seq, HEAD_DIM), BF16),
                        pltpu.VMEM((seq + ATTN_BLOCK, HEAD_DIM), BF16),
                        pltpu.VMEM((seq + ATTN_BLOCK, HEAD_DIM), BF16),
                        pltpu.VMEM((nb, seq, HEAD_DIM), F32),
                        pltpu.VMEM((nb, seq, HEAD_DIM), F32)],
        compiler_params=_params("arbitrary", "arbitrary"),
        name="dilated_attention",
    )(proj, proj, proj)


def _mix_kernel(attn_ref, u_ref, wp_ref, ps_ref, ga_ref, gp_ref, o_ref, ext, *, tr, aw, gd):
    j = pl.program_id(1)
    n_groups = len(POOL_WINDOWS)

    @pl.when(j == 0)
    def _():
        ext[0:POOL_HALO, :] = jnp.zeros((POOL_HALO, ext.shape[1]), BF16)

    @pl.when(j > 0)
    def _():
        ext[0:POOL_HALO, :] = ext[tr:tr + POOL_HALO, :]

    ext[POOL_HALO:POOL_HALO + tr, :] = u_ref[...]

    o_ref[:, 0:aw] = _rms(attn_ref[...].astype(F32), ga_ref[...]).astype(o_ref.dtype)

    pos = j * tr + lax.broadcasted_iota(jnp.int32, (tr, 1), 0)
    diff = (lax.broadcasted_iota(jnp.int32, (tr, tr + POOL_HALO), 0) + POOL_HALO
            - lax.broadcasted_iota(jnp.int32, (tr, tr + POOL_HALO), 1))
    ys = []
    ssq = jnp.zeros((tr, 1), F32)
    for g, w in enumerate(POOL_WINDOWS):
        band = ((diff >= 0) & (diff < w)).astype(BF16)
        win = jnp.dot(band, ext[:, g * gd:(g + 1) * gd], preferred_element_type=F32)
        cnt = jnp.minimum(pos + 1, w).astype(F32)
        mix = win * (1.0 / cnt) - u_ref[:, g * gd:(g + 1) * gd].astype(F32)
        y = jnp.dot(mix.astype(BF16), wp_ref[g], preferred_element_type=F32) * ps_ref[g:g + 1, :]
        ys.append(y)
        ssq = ssq + jnp.sum(y * y, axis=1, keepdims=True)
    r = lax.rsqrt(ssq * (1.0 / (n_groups * gd)) + NORM_EPS)
    for g in range(n_groups):
        o_ref[:, aw + g * gd:aw + (g + 1) * gd] = (
            ys[g] * r * gp_ref[:, g * gd:(g + 1) * gd]).astype(o_ref.dtype)


def _mixer_norms(attn, proj, w_pool_bf, pool_scale, g_attn, g_pool, batch, seq):
    t, aw = attn.shape
    n_groups, gd, _ = w_pool_bf.shape
    pw = n_groups * gd
    tr = min(256, seq)
    tiles = seq // tr
    u_col_block = (proj.shape[1] - pw) // pw
    return pl.pallas_call(
        functools.partial(_mix_kernel, tr=tr, aw=aw, gd=gd),
        grid=(batch, tiles),
        in_specs=[pl.BlockSpec((tr, aw), lambda b, j: (b * tiles + j, 0)),
                  pl.BlockSpec((tr, pw), lambda b, j: (b * tiles + j, u_col_block)),
                  pl.BlockSpec((n_groups, gd, gd), lambda b, j: (0, 0, 0)),
                  pl.BlockSpec((n_groups, gd), lambda b, j: (0, 0)),
                  pl.BlockSpec((1, aw), lambda b, j: (0, 0)),
                  pl.BlockSpec((1, pw), lambda b, j: (0, 0))],
        out_specs=pl.BlockSpec((tr, aw + pw), lambda b, j: (b * tiles + j, 0)),
        out_shape=jax.ShapeDtypeStruct((t, aw + pw), BF16),
        scratch_shapes=[pltpu.VMEM((tr + POOL_HALO, pw), BF16)],
        compiler_params=_params("arbitrary", "arbitrary"),
        name="mixer_norms",
    )(attn, proj, w_pool_bf, pool_scale, g_attn.reshape(1, aw), g_pool.reshape(1, pw))


def _matmul_kernel(a_ref, b_ref, o_ref):
    o_ref[...] = jnp.dot(a_ref[...], b_ref[...], preferred_element_type=F32).astype(o_ref.dtype)


def _matmul(a, b, out_dtype, tm=1024, tn=1024):
    m, k = a.shape
    n = b.shape[1]
    tm, tn = min(tm, m), min(tn, n)
    return pl.pallas_call(
        _matmul_kernel,
        grid=(m // tm, n // tn),
        in_specs=[pl.BlockSpec((tm, k), lambda i, j: (i, 0)),
                  pl.BlockSpec((k, tn), lambda i, j: (0, j))],
        out_specs=pl.BlockSpec((tm, tn), lambda i, j: (i, j)),
        out_shape=jax.ShapeDtypeStruct((m, n), out_dtype),
        compiler_params=_params("arbitrary", "arbitrary"),
        name="out_projection",
    )(a, b)


def _postmix_kernel(y_ref, x_ref, mod_ref, gpost_ref, gpre_ref, h_ref, f_ref):
    gate1 = mod_ref[0, 2:3, :]
    shift2 = mod_ref[0, 3:4, :]
    scale2 = mod_ref[0, 4:5, :]
    h = x_ref[...] + gate1 * _rms(y_ref[...], gpost_ref[...])
    h_ref[...] = h
    f_ref[...] = _rms(h, gpre_ref[...]) * (1.0 + scale2) + shift2


def _post_mix(y, x2, mod, g_post, g_pre, seq):
    t, d = x2.shape
    tm = min(256, seq)
    tiles_per_seq = seq // tm
    row = pl.BlockSpec((tm, d), lambda i: (i, 0))
    vec = pl.BlockSpec((1, d), lambda i: (0, 0))
    return pl.pallas_call(
        _postmix_kernel,
        grid=(t // tm,),
        in_specs=[row, row,
                  pl.BlockSpec((1, N_MOD, d), lambda i: (i // tiles_per_seq, 0, 0)),
                  vec, vec],
        out_specs=[row, row],
        out_shape=[jax.ShapeDtypeStruct((t, d), F32), jax.ShapeDtypeStruct((t, d), F32)],
        compiler_params=_params("arbitrary"),
        name="post_mix",
    )(y, x2, mod, g_post.reshape(1, d), g_pre.reshape(1, d))


def _split_bf16(x):
    hi = x.astype(BF16)
    lo = (x - hi.astype(F32)).astype(BF16)
    return hi, lo


def _router_kernel(f_ref, wt_ref, bias_ref, idx_ref, wts_ref):
    f_hi, f_lo = _split_bf16(f_ref[...])
    w_hi, w_lo = _split_bf16(wt_ref[...])
    logits = (lax.dot_general(w_hi, f_hi, _NT, preferred_element_type=F32)
              + lax.dot_general(w_hi, f_lo, _NT, preferred_element_type=F32)
              + lax.dot_general(w_lo, f_hi, _NT, preferred_element_type=F32))
    n_exp, tm = logits.shape
    per = n_exp // N_EXPERT_GROUPS
    scores = jax.nn.sigmoid(logits)
    biased = scores + bias_ref[...]
    ninf = -jnp.inf

    b3 = biased.reshape(N_EXPERT_GROUPS, per, tm)
    memb = lax.broadcasted_iota(jnp.int32, b3.shape, 1)
    m1 = jnp.max(b3, axis=1, keepdims=True)
    i1 = jnp.min(jnp.where(b3 == m1, memb, per), axis=1, keepdims=True)
    m2 = jnp.max(jnp.where(memb == i1, ninf, b3), axis=1, keepdims=True)
    gscore = (m1 + m2).reshape(N_EXPERT_GROUPS, tm)

    gid = lax.broadcasted_iota(jnp.int32, gscore.shape, 0)
    gmask = jnp.zeros(gscore.shape, jnp.bool_)
    cur = gscore
    for _ in range(TOPK_GROUPS):
        mx = jnp.max(cur, axis=0, keepdims=True)
        ix = jnp.min(jnp.where(cur == mx, gid, N_EXPERT_GROUPS), axis=0, keepdims=True)
        sel = gid == ix
        gmask = gmask | sel
        cur = jnp.where(sel, ninf, cur)
    emask = jnp.broadcast_to(gmask.reshape(N_EXPERT_GROUPS, 1, tm), b3.shape).reshape(n_exp, tm)

    masked = jnp.where(emask, biased, ninf)
    eid = lax.broadcasted_iota(jnp.int32, masked.shape, 0)
    idx_rows, w_rows = [], []
    for _ in range(TOP_K):
        mx = jnp.max(masked, axis=0, keepdims=True)
        ix = jnp.min(jnp.where(masked == mx, eid, n_exp), axis=0, keepdims=True)
        sel = eid == ix
        idx_rows.append(ix)
        w_rows.append(jnp.sum(jnp.where(sel, scores, 0.0), axis=0, keepdims=True))
        masked = jnp.where(sel, ninf, masked)
    wsel = jnp.concatenate(w_rows, axis=0)
    idx_ref[...] = jnp.concatenate(idx_rows, axis=0)
    wts_ref[...] = wsel / jnp.sum(wsel, axis=0, keepdims=True) * ROUTED_SCALE


def _router(f, w_router, router_bias):
    t, d = f.shape
    n_exp = w_router.shape[1]
    tm = min(512, t)
    return pl.pallas_call(
        _router_kernel,
        grid=(t // tm,),
        in_specs=[pl.BlockSpec((tm, d), lambda i: (i, 0)),
                  pl.BlockSpec((n_exp, d), lambda i: (0, 0)),
                  pl.BlockSpec((n_exp, 1), lambda i: (0, 0))],
        out_specs=[pl.BlockSpec((TOP_K, tm), lambda i: (0, i)),
                   pl.BlockSpec((TOP_K, tm), lambda i: (0, i))],
        out_shape=[jax.ShapeDtypeStruct((TOP_K, t), jnp.int32),
                   jax.ShapeDtypeStruct((TOP_K, t), F32)],
        compiler_params=_params("arbitrary"),
        name="router",
    )(f, w_router.T, router_bias.reshape(n_exp, 1))


def _dispatch_tables(idx_t, wts_t, n_exp, rows_per_block):
    k, t = idx_t.shape
    n = k * t
    n_blocks = n // rows_per_block + n_exp
    rows = n_blocks * rows_per_block
    chosen = jnp.zeros((t, n_exp), jnp.int32).at[jnp.arange(t)[None, :], idx_t].add(1)
    rank = jnp.cumsum(chosen, axis=0) - chosen
    counts = jnp.sum(chosen, axis=0)
    padded = (counts + rows_per_block - 1) // rows_per_block * rows_per_block
    pad_ends = jnp.cumsum(padded)
    pad_starts = pad_ends - padded
    dest = pad_starts[idx_t] + jnp.take_along_axis(rank, idx_t.T, axis=1).T
    dest = dest.reshape(n)
    tok = jnp.broadcast_to(jnp.arange(t, dtype=jnp.int32)[None, :], (k, t)).reshape(n)
    slot_row = jnp.arange(n, dtype=jnp.int32)
    row_tok = jnp.zeros((rows,), jnp.int32).at[dest].set(tok)
    row_dst = jnp.zeros((rows,), jnp.int32).at[dest].set(slot_row)
    row_w = jnp.zeros((rows,), F32).at[dest].set(wts_t.reshape(n))
    n_used = (pad_ends[-1] // rows_per_block).astype(jnp.int32)
    blk_start = jnp.arange(n_blocks, dtype=jnp.int32) * rows_per_block
    block_e = jnp.minimum(jnp.searchsorted(pad_ends, blk_start, side='right'), n_exp - 1).astype(jnp.int32)
    n_real = jnp.clip(pad_starts[block_e] + counts[block_e] - blk_start, 0, rows_per_block).astype(jnp.int32)
    prev_e = jnp.concatenate([jnp.full((1,), -1, jnp.int32), block_e[:-1]])
    first = (block_e != prev_e).astype(jnp.int32)
    after = (pad_ends[block_e] // rows_per_block).astype(jnp.int32)
    next_e = jnp.where(after < n_used, block_e[jnp.minimum(after, n_blocks - 1)], -1).astype(jnp.int32)
    meta = jnp.concatenate([row_tok.reshape(n_blocks, rows_per_block),
                            row_dst.reshape(n_blocks, rows_per_block)], axis=1)
    return block_e, first, next_e, n_real, n_used.reshape(1), meta, row_w.reshape(rows, 1)


def _expert_kernel(be_ref, first_ref, nexte_ref, nreal_ref, nused_ref,
                   meta_hbm, roww_ref, f_hbm, wg_hbm, wu_hbm, wd_hbm,
                   ys_hbm,
                   meta_s, xs, ob, pad_sink, stage_g, stage_u, stage_d, wg, wu, wd,
                   sem_meta, sem_x, sem_o, sem_w, *, rows, n_blocks):
    i = pl.program_id(0)
    n_used = nused_ref[0]
    xslot = i % 2

    def meta_copy(blk, s3):
        return pltpu.make_async_copy(meta_hbm.at[blk], meta_s.at[s3], sem_meta.at[s3])

    def gather_start(s3, slot):
        def body(r, c):
            tok = meta_s[s3, r]
            pltpu.make_async_copy(f_hbm.at[pl.ds(tok, 1), :], xs.at[slot, pl.ds(r, 1), :],
                                  sem_x.at[slot]).start()
            return c
        lax.fori_loop(0, rows, body, 0, unroll=8)

    def gather_wait(slot):
        pltpu.make_async_copy(f_hbm.at[pl.ds(0, rows), :], xs.at[slot], sem_x.at[slot]).wait()

    def scatter_start(s3, slot, n_real):
        def body(r, c):
            dst = meta_s[s3, rows + r]
            pltpu.make_async_copy(ob.at[slot, pl.ds(r, 1), :], ys_hbm.at[pl.ds(dst, 1), :],
                                  sem_o.at[slot]).start()
            return c
        lax.fori_loop(0, n_real, body, 0)

        def pad_body(r, c):
            pltpu.make_async_copy(ob.at[slot, pl.ds(r, 1), :], pad_sink.at[slot, pl.ds(r, 1), :],
                                  sem_o.at[slot]).start()
            return c
        lax.fori_loop(n_real, rows, pad_body, 0)

    def scatter_wait(slot):
        pltpu.make_async_copy(ob.at[slot], ys_hbm.at[pl.ds(0, rows), :], sem_o.at[slot]).wait()

    def weight_copies(e):
        return (pltpu.make_async_copy(wg_hbm.at[e], stage_g, sem_w.at[0]),
                pltpu.make_async_copy(wu_hbm.at[e], stage_u, sem_w.at[1]),
                pltpu.make_async_copy(wd_hbm.at[e], stage_d, sem_w.at[2]))

    def cast_weights(src, dst):
        chunk = math.gcd(256, src.shape[0])
        def body(c, carry):
            r0 = pl.multiple_of(c * chunk, chunk)
            dst[pl.ds(r0, chunk), :] = src[pl.ds(r0, chunk), :].astype(BF16)
            return carry
        lax.fori_loop(0, src.shape[0] // chunk, body, 0)

    @pl.when((i == 0) & (n_used > 0))
    def _():
        first_meta = meta_copy(0, 0)
        first_meta.start()
        first_meta.wait()
        gather_start(0, 0)
        for cp in weight_copies(be_ref[0]):
            cp.start()

        @pl.when(n_used > 1)
        def _():
            meta_copy(1, 1).start()

    @pl.when(i < n_used)
    def _():
        s3 = i % 3

        @pl.when(i + 1 < n_used)
        def _():
            meta_copy(i + 1, (i + 1) % 3).wait()
            gather_start((i + 1) % 3, 1 - xslot)

        @pl.when(i + 2 < n_used)
        def _():
            meta_copy(i + 2, (i + 2) % 3).start()

        @pl.when(first_ref[i] == 1)
        def _():
            for cp in weight_copies(be_ref[i]):
                cp.wait()
            cast_weights(stage_g, wg)
            cast_weights(stage_u, wu)
            cast_weights(stage_d, wd)

            @pl.when(nexte_ref[i] >= 0)
            def _():
                for cp in weight_copies(nexte_ref[i]):
                    cp.start()

        gather_wait(xslot)
        x = xs[xslot].astype(BF16)
        g = jnp.dot(x, wg[...], preferred_element_type=F32)
        u = jnp.dot(x, wu[...], preferred_element_type=F32)
        act = (_silu(g) * u).astype(BF16)
        y = jnp.dot(act, wd[...], preferred_element_type=F32) * roww_ref[...]

        @pl.when(i >= 2)
        def _():
            scatter_wait(xslot)

        ob[xslot] = y
        scatter_start(s3, xslot, nreal_ref[i])

    @pl.when(i == n_blocks - 1)
    def _():
        @pl.when(n_used >= 1)
        def _():
            scatter_wait((n_used - 1) % 2)

        @pl.when(n_used >= 2)
        def _():
            scatter_wait(n_used % 2)


def _routed_experts(f, tables, w_gate, w_up, w_down, top_k):
    block_e, first, next_e, n_real, n_used, meta, row_w = tables
    t, d = f.shape
    n_exp, _, de = w_gate.shape
    rows = MOE_ROWS
    n_blocks = block_e.shape[0]
    any_spec = pl.BlockSpec(memory_space=pl.ANY)
    grid_spec = pltpu.PrefetchScalarGridSpec(
        num_scalar_prefetch=5,
        grid=(n_blocks,),
        in_specs=[any_spec,
                  pl.BlockSpec((rows, 1), lambda i, *_: (i, 0)),
                  any_spec, any_spec, any_spec, any_spec],
        out_specs=any_spec,
        scratch_shapes=[pltpu.SMEM((3, 2 * rows), jnp.int32),
                        pltpu.VMEM((2, rows, d), F32),
                        pltpu.VMEM((2, rows, d), F32),
                        pltpu.VMEM((2, rows, d), F32),
                        pltpu.VMEM((d, de), F32),
                        pltpu.VMEM((d, de), F32),
                        pltpu.VMEM((de, d), F32),
                        pltpu.VMEM((d, de), BF16),
                        pltpu.VMEM((d, de), BF16),
                        pltpu.VMEM((de, d), BF16),
                        pltpu.SemaphoreType.DMA((3,)),
                        pltpu.SemaphoreType.DMA((2,)),
                        pltpu.SemaphoreType.DMA((2,)),
                        pltpu.SemaphoreType.DMA((3,))])
    return pl.pallas_call(
        functools.partial(_expert_kernel, rows=rows, n_blocks=n_blocks),
        grid_spec=grid_spec,
        out_shape=jax.ShapeDtypeStruct((top_k * t, d), F32),
        compiler_params=_params("arbitrary"),
        name="routed_experts",
    )(block_e, first, next_e, n_real, n_used, meta, row_w, f, w_gate, w_up, w_down)


def _shared_kernel(f_ref, wg_ref, wu_ref, wd_ref, o_ref):
    x = f_ref[...].astype(BF16)
    g = jnp.dot(x, wg_ref[...], preferred_element_type=F32)
    u = jnp.dot(x, wu_ref[...], preferred_element_type=F32)
    act = (_silu(g) * u).astype(BF16)
    o_ref[...] = jnp.dot(act, wd_ref[...], preferred_element_type=F32).astype(o_ref.dtype)


def _shared_expert(f, wg_bf, wu_bf, wd_bf):
    t, d = f.shape
    ds = wg_bf.shape[1]
    tm = min(512, t)
    return pl.pallas_call(
        _shared_kernel,
        grid=(t // tm,),
        in_specs=[pl.BlockSpec((tm, d), lambda i: (i, 0)),
                  pl.BlockSpec((d, ds), lambda i: (0, 0)),
                  pl.BlockSpec((d, ds), lambda i: (0, 0)),
                  pl.BlockSpec((ds, d), lambda i: (0, 0))],
        out_specs=pl.BlockSpec((tm, d), lambda i: (i, 0)),
        out_shape=jax.ShapeDtypeStruct((t, d), BF16),
        compiler_params=_params("arbitrary"),
        name="shared_expert",
    )(f, wg_bf, wu_bf, wd_bf)


def _combine_kernel(*refs, top_k):
    ys_refs = refs[:top_k]
    sh_ref, h_ref, mod_ref, g_ref, o_ref = refs[top_k:]
    moe = sh_ref[...].astype(F32)
    routed = ys_refs[0][...]
    for r in ys_refs[1:]:
        routed = routed + r[...]
    moe = routed + moe
    gate2 = mod_ref[0, 5:6, :]
    o_ref[...] = h_ref[...] + gate2 * _rms(moe, g_ref[...])


def _combine(ys, shared, h, mod, g_post, seq, top_k):
    t, d = h.shape
    tm = min(128, seq)
    tiles = t // tm
    tiles_per_seq = seq // tm
    row = pl.BlockSpec((tm, d), lambda i: (i, 0))
    slot_specs = [pl.BlockSpec((tm, d), lambda i, k=k: (k * tiles + i, 0)) for k in range(top_k)]
    return pl.pallas_call(
        functools.partial(_combine_kernel, top_k=top_k),
        grid=(tiles,),
        in_specs=slot_specs + [row, row,
                               pl.BlockSpec((1, N_MOD, d), lambda i: (i // tiles_per_seq, 0, 0)),
                               pl.BlockSpec((1, d), lambda i: (0, 0))],
        out_specs=row,
        out_shape=jax.ShapeDtypeStruct((t, d), F32),
        compiler_params=_params("arbitrary"),
        name="combine",
    )(*([ys] * top_k), shared, h, mod, g_post.reshape(1, d))


def _rope_tables(seq):
    half = HEAD_DIM // 2
    inv_freq = jnp.exp(-math.log(ROPE_THETA) * jnp.arange(half, dtype=F32) / half)
    ang = jnp.arange(seq, dtype=F32)[:, None] * inv_freq[None, :]
    cos, sin = jnp.cos(ang), jnp.sin(ang)
    return jnp.concatenate([cos, cos], axis=1), jnp.concatenate([-sin, sin], axis=1)


def kernel(x, c, w_mod, b_mod, g_pre_mix, g_post_mix, g_pre_ffn, g_post_ffn, w_in, w_pool, pool_scale, g_attn_out, g_pool_out, w_out, w_router, router_bias, w_gate, w_up, w_down, w_shared_gate, w_shared_up, w_shared_down):
    batch, seq, d = x.shape
    depth = w_mod.shape[0]
    n_groups, gd = w_pool.shape[1], w_pool.shape[2]
    pool_width = n_groups * gd
    attn_width = (w_in.shape[2] - pool_width) // 3
    n_heads = attn_width // HEAD_DIM
    n_exp = w_router.shape[2]
    assert n_groups == len(POOL_WINDOWS) and seq % (ATTN_BLOCK * DILATIONS[-1]) == 0
    cosf, sinf = _rope_tables(seq)

    h = x.reshape(batch * seq, d)
    for layer in range(depth):
        mod = _modulation(c, w_mod[layer], b_mod[layer])
        proj = _in_projection(h, g_pre_mix[layer], mod, w_in[layer].astype(BF16), cosf, sinf,
                              seq, 2 * attn_width)
        attn = _dilated_attention(proj, batch, seq, n_heads)
        mixed = _mixer_norms(attn, proj, w_pool[layer].astype(BF16), pool_scale[layer],
                             g_attn_out[layer], g_pool_out[layer], batch, seq)
        y = _matmul(mixed, w_out[layer].astype(BF16), F32)
        h, f = _post_mix(y, h, mod, g_post_mix[layer], g_pre_ffn[layer], seq)
        idx_t, wts_t = _router(f, w_router[layer], router_bias[layer])
        tables = _dispatch_tables(idx_t, wts_t, n_exp, MOE_ROWS)
        ys = _routed_experts(f, tables, w_gate[layer], w_up[layer], w_down[layer], TOP_K)
        shared = _shared_expert(f, w_shared_gate[layer].astype(BF16),
                                w_shared_up[layer].astype(BF16), w_shared_down[layer].astype(BF16))
        h = _combine(ys, shared, h, mod, g_post_ffn[layer], seq, TOP_K)
    return h.reshape(batch, seq, d)
```

```python
import functools
import math

import jax
import jax.numpy as jnp
from jax import lax
from jax.experimental import pallas as pl
from jax.experimental.pallas import tpu as pltpu

F32 = jnp.float32
BF16 = jnp.bfloat16
I32 = jnp.int32
U32 = jnp.uint32

HEAD_DIM = 128
ATTN_BLOCK = 128
DILATIONS = (1, 4, 16)
POOL_WINDOWS = (2, 4, 8, 16)
POOL_HALO = 16
ROPE_THETA = 10000.0
TOP_K = 8
N_EXPERT_GROUPS = 8
TOPK_GROUPS = 4
ROUTED_SCALE = 2.5
N_MOD = 6
NORM_EPS = 1e-6
MASK_VALUE = -1e30

LANES = 128
SUBLANES = 8
VMEM_LIMIT_BYTES = 60 * 1024 * 1024

MOE_ROWS = 256
EXPERT_LOOKAHEAD = 2
COMBINE_TOKENS = 128

_NT = (((1,), (1,)), ((), ()))


def _params(*sem):
    return pltpu.CompilerParams(dimension_semantics=sem, vmem_limit_bytes=VMEM_LIMIT_BYTES)


def _rms(x, g):
    return x * lax.rsqrt(jnp.mean(x * x, axis=-1, keepdims=True) + NORM_EPS) * g


def _silu(x):
    return x * jax.nn.sigmoid(x)


def _pack_bf16_pair(x):
    n = x.shape[1] // 2
    lo = lax.bitcast_convert_type(x[:, :n].astype(BF16).astype(F32), U32)
    hi = lax.bitcast_convert_type(x[:, n:].astype(BF16).astype(F32), U32)
    return (lo >> 16) | hi


def _unpack_bf16_pair(w):
    lo = lax.bitcast_convert_type(w << 16, F32)
    hi = lax.bitcast_convert_type(w & jnp.uint32(0xFFFF0000), F32)
    return lo, hi


def _mod_kernel(c_ref, w_ref, b_ref, o_ref):
    cond = _silu(c_ref[...])
    o_ref[...] = jnp.dot(cond.astype(BF16), w_ref[...].astype(BF16),
                         preferred_element_type=F32) + b_ref[...]


def _modulation(c, w_mod, b_mod):
    b, d = c.shape
    n = w_mod.shape[1]
    bp = -(-b // SUBLANES) * SUBLANES
    tn = 512
    out = pl.pallas_call(
        _mod_kernel,
        grid=(n // tn,),
        in_specs=[pl.BlockSpec((bp, d), lambda j: (0, 0)),
                  pl.BlockSpec((d, tn), lambda j: (0, j)),
                  pl.BlockSpec((1, tn), lambda j: (0, j))],
        out_specs=pl.BlockSpec((bp, tn), lambda j: (0, j)),
        out_shape=jax.ShapeDtypeStruct((bp, n), F32),
        compiler_params=_params("arbitrary"),
        name="modulation",
    )(jnp.pad(c, ((0, bp - b), (0, 0))), w_mod, b_mod.reshape(1, n))
    return out[:b].reshape(b, N_MOD, d)


def _inproj_kernel(x_ref, g_ref, mod_ref, w_ref, cos_ref, sin_ref, o_ref, a_scr, *,
                   n_q_tiles, n_rope_tiles):
    j = pl.program_id(1)

    @pl.when(j == 0)
    def _():
        shift1 = mod_ref[0, 0:1, :]
        scale1 = mod_ref[0, 1:2, :]
        a = _rms(x_ref[...], g_ref[...]) * (1.0 + scale1) + shift1
        a_scr[...] = a.astype(BF16)

    acc = jnp.dot(a_scr[...], w_ref[...], preferred_element_type=F32)

    @pl.when(j < n_rope_tiles)
    def _():
        qk_scale = jnp.where(j < n_q_tiles, HEAD_DIM ** -0.5, 1.0)
        cosf = cos_ref[...] * qk_scale
        sinf = sin_ref[...] * qk_scale
        for hh in range(acc.shape[1] // HEAD_DIM):
            blk = acc[:, hh * HEAD_DIM:(hh + 1) * HEAD_DIM]
            rot = blk * cosf + pltpu.roll(blk, HEAD_DIM // 2, 1) * sinf
            o_ref[:, hh * HEAD_DIM:(hh + 1) * HEAD_DIM] = rot.astype(o_ref.dtype)

    @pl.when(j >= n_rope_tiles)
    def _():
        o_ref[...] = acc.astype(o_ref.dtype)


def _in_projection(x2, g, mod, w_in_bf, cosf, sinf, seq, attn_width):
    t, d = x2.shape
    n = w_in_bf.shape[1]
    tm = min(512, seq)
    tn = min(1024, attn_width)
    tiles_per_seq = seq // tm
    return pl.pallas_call(
        functools.partial(_inproj_kernel, n_q_tiles=attn_width // tn, n_rope_tiles=2 * attn_width // tn),
        grid=(t // tm, n // tn),
        in_specs=[pl.BlockSpec((tm, d), lambda i, j: (i, 0)),
                  pl.BlockSpec((1, d), lambda i, j: (0, 0)),
                  pl.BlockSpec((1, N_MOD, d), lambda i, j: (i // tiles_per_seq, 0, 0)),
                  pl.BlockSpec((d, tn), lambda i, j: (0, j)),
                  pl.BlockSpec((tm, HEAD_DIM), lambda i, j: (i % tiles_per_seq, 0)),
                  pl.BlockSpec((tm, HEAD_DIM), lambda i, j: (i % tiles_per_seq, 0))],
        out_specs=pl.BlockSpec((tm, tn), lambda i, j: (i, j)),
        out_shape=jax.ShapeDtypeStruct((t, n), BF16),
        scratch_shapes=[pltpu.VMEM((tm, d), BF16)],
        compiler_params=_params("arbitrary", "arbitrary"),
        name="in_projection",
    )(x2, g.reshape(1, d), mod, w_in_bf, cosf, sinf)


def _attn_kernel(q_ref, k_ref, v_ref, o_ref, qf, kf, vf, qs, ks, vs, accs, maxs, dens, stage, *, seq):
    blk = ATTN_BLOCK
    n_blk = seq // blk
    qf[...] = q_ref[...].astype(F32)
    kf[...] = k_ref[...].astype(F32)
    vf[...] = v_ref[...].astype(F32)
    vs[:, HEAD_DIM:2 * HEAD_DIM] = jnp.ones((vs.shape[0], HEAD_DIM), BF16)

    qi = lax.broadcasted_iota(I32, (blk, blk), 0)
    ki = lax.broadcasted_iota(I32, (blk, blk), 1)
    causal_bias = jnp.where(ki <= qi, 0.0, MASK_VALUE)
    upper_bias = jnp.where(ki >= qi, 0.0, MASK_VALUE)

    for bi, dil in enumerate(DILATIONS):
        sub_len = seq // dil
        nbk = sub_len // blk
        lead = blk if nbk > 1 else 0
        pitch = sub_len + lead
        def to_sub_order(res, carry, dil=dil, sub_len=sub_len, lead=lead, pitch=pitch):
            base = pl.multiple_of(jnp.asarray(res * pitch, I32), blk)
            q0 = pl.multiple_of(jnp.asarray(res * sub_len, I32), blk)
            if dil == 1:
                q_src, k_src, v_src = q_ref[...], k_ref[...], v_ref[...]
            else:
                q_src = qf[pl.ds(res, sub_len, stride=dil), :].astype(BF16)
                k_src = kf[pl.ds(res, sub_len, stride=dil), :].astype(BF16)
                v_src = vf[pl.ds(res, sub_len, stride=dil), :].astype(BF16)
            qs[pl.ds(q0, sub_len), :] = q_src
            if lead:
                ks[pl.ds(base, lead), :] = jnp.zeros((lead, HEAD_DIM), BF16)
                vs[pl.ds(base, lead), 0:HEAD_DIM] = jnp.zeros((lead, HEAD_DIM), BF16)
            ks[pl.ds(base + lead, sub_len), :] = k_src
            vs[pl.ds(base + lead, sub_len), 0:HEAD_DIM] = v_src
            return carry

        if dil == 1:
            to_sub_order(0, 0)
        else:
            lax.fori_loop(0, dil, to_sub_order, 0)

        def blocks(ref, offset, dil=dil, pitch=pitch, sub_len=sub_len, nbk=nbk):
            parts = [ref[res * pitch + offset:res * pitch + offset + sub_len, :]
                     .reshape(nbk, blk, ref.shape[1]) for res in range(dil)]
            return parts[0] if dil == 1 else jnp.concatenate(parts, axis=0)

        q3 = qs[...].reshape(n_blk, blk, HEAD_DIM)
        s_own = jnp.einsum('nqc,nkc->nqk', q3, blocks(ks, lead),
                           preferred_element_type=F32) + causal_bias[None]
        if nbk > 1:
            first = (lax.broadcasted_iota(I32, (n_blk, blk, blk), 0) & (nbk - 1)) == 0
            prev_bias = jnp.where(first, MASK_VALUE, upper_bias[None])
            s_prev = jnp.einsum('nqc,nkc->nqk', q3, blocks(ks, 0),
                                preferred_element_type=F32) + prev_bias
            m = jnp.max(jnp.maximum(s_own, s_prev), axis=-1, keepdims=True)
            accx = (jnp.einsum('nqk,nkc->nqc', jnp.exp(s_own - m).astype(BF16), blocks(vs, lead),
                               preferred_element_type=F32)
                    + jnp.einsum('nqk,nkc->nqc', jnp.exp(s_prev - m).astype(BF16), blocks(vs, 0),
                                 preferred_element_type=F32))
        else:
            m = jnp.max(s_own, axis=-1, keepdims=True)
            accx = jnp.einsum('nqk,nkc->nqc', jnp.exp(s_own - m).astype(BF16), blocks(vs, lead),
                              preferred_element_type=F32)
        m_b = jnp.broadcast_to(m, (n_blk, blk, HEAD_DIM))
        results = (accx[:, :, 0:HEAD_DIM], accx[:, :, HEAD_DIM:2 * HEAD_DIM], m_b)
        if dil == 1:
            for dst, val in zip((accs, dens, maxs), results):
                dst[bi] = val.reshape(seq, HEAD_DIM)
        else:
            for j, val in enumerate(results):
                stage[j] = val.reshape(seq, HEAD_DIM)

            def to_natural_order(res, carry, dil=dil, sub_len=sub_len, bi=bi):
                r0 = pl.multiple_of(jnp.asarray(res * sub_len, I32), blk)
                for j, dst in enumerate((accs, dens, maxs)):
                    dst[bi, pl.ds(res, sub_len, stride=dil), :] = stage[j, pl.ds(r0, sub_len), :]
                return carry

            lax.fori_loop(0, dil, to_natural_order, 0)

    def merge(c, carry):
        r0 = pl.multiple_of(c * blk, blk)
        ms = [maxs[bi, pl.ds(r0, blk), :] for bi in range(len(DILATIONS))]
        m_all = functools.reduce(jnp.maximum, ms)
        ws = [jnp.exp(m - m_all) for m in ms]
        den = sum(w * dens[bi, pl.ds(r0, blk), :] for bi, w in enumerate(ws))
        num = sum(w * accs[bi, pl.ds(r0, blk), :] for bi, w in enumerate(ws))
        o_ref[pl.ds(r0, blk), :] = (num / den).astype(o_ref.dtype)
        return carry

    lax.fori_loop(0, n_blk, merge, 0, unroll=2)


def _dilated_attention(proj, batch, seq, n_heads):
    t = proj.shape[0]
    nb = len(DILATIONS)
    key_rows = max(d * (seq // d + (ATTN_BLOCK if seq // d > ATTN_BLOCK else 0)) for d in DILATIONS)
    return pl.pallas_call(
        functools.partial(_attn_kernel, seq=seq),
        grid=(batch, n_heads),
        in_specs=[pl.BlockSpec((seq, HEAD_DIM), lambda b, h: (b, h)),
                  pl.BlockSpec((seq, HEAD_DIM), lambda b, h: (b, n_heads + h)),
                  pl.BlockSpec((seq, HEAD_DIM), lambda b, h: (b, 2 * n_heads + h))],
        out_specs=pl.BlockSpec((seq, HEAD_DIM), lambda b, h: (b, h)),
        out_shape=jax.ShapeDtypeStruct((t, n_heads * HEAD_DIM), BF16),
        scratch_shapes=[pltpu.VMEM((seq, HEAD_DIM), F32),
                        pltpu.VMEM((seq, HEAD_DIM), F32),
                        pltpu.VMEM((seq, HEAD_DIM), F32),
                        pltpu.VMEM((seq, HEAD_DIM), BF16),
                        pltpu.VMEM((key_rows, HEAD_DIM), BF16),
                        pltpu.VMEM((key_rows, 2 * HEAD_DIM), BF16),
                        pltpu.VMEM((nb, seq, HEAD_DIM), F32),
                        pltpu.VMEM((nb, seq, HEAD_DIM), F32),
                        pltpu.VMEM((nb, seq, HEAD_DIM), F32),
                        pltpu.VMEM((3, seq, HEAD_DIM), F32)],
        compiler_params=_params("arbitrary", "arbitrary"),
        name="dilated_attention",
    )(proj, proj, proj)


def _mix_kernel(attn_ref, u_ref, wp_ref, ps_ref, ga_ref, gp_ref, o_ref, ext, *, tr, aw, gd):
    j = pl.program_id(1)
    n_groups = len(POOL_WINDOWS)

    @pl.when(j == 0)
    def _():
        ext[0:POOL_HALO, :] = jnp.zeros((POOL_HALO, ext.shape[1]), BF16)

    @pl.when(j > 0)
    def _():
        ext[0:POOL_HALO, :] = ext[tr:tr + POOL_HALO, :]

    ext[POOL_HALO:POOL_HALO + tr, :] = u_ref[...]

    o_ref[:, 0:aw] = _rms(attn_ref[...].astype(F32), ga_ref[...]).astype(o_ref.dtype)

    pos = j * tr + lax.broadcasted_iota(I32, (tr, 1), 0)
    diff = (lax.broadcasted_iota(I32, (tr, tr + POOL_HALO), 0) + POOL_HALO
            - lax.broadcasted_iota(I32, (tr, tr + POOL_HALO), 1))
    ys = []
    ssq = jnp.zeros((tr, 1), F32)
    for g, w in enumerate(POOL_WINDOWS):
        band = ((diff >= 0) & (diff < w)).astype(BF16)
        win = jnp.dot(band, ext[:, g * gd:(g + 1) * gd], preferred_element_type=F32)
        cnt = jnp.minimum(pos + 1, w).astype(F32)
        mix = win * (1.0 / cnt) - u_ref[:, g * gd:(g + 1) * gd].astype(F32)
        y = jnp.dot(mix.astype(BF16), wp_ref[g], preferred_element_type=F32) * ps_ref[g:g + 1, :]
        ys.append(y)
        ssq = ssq + jnp.sum(y * y, axis=1, keepdims=True)
    r = lax.rsqrt(ssq * (1.0 / (n_groups * gd)) + NORM_EPS)
    for g in range(n_groups):
        o_ref[:, aw + g * gd:aw + (g + 1) * gd] = (
            ys[g] * r * gp_ref[:, g * gd:(g + 1) * gd]).astype(o_ref.dtype)


def _mixer_norms(attn, proj, w_pool_bf, pool_scale, g_attn, g_pool, batch, seq):
    t, aw = attn.shape
    n_groups, gd, _ = w_pool_bf.shape
    pw = n_groups * gd
    tr = min(256, seq)
    tiles = seq // tr
    u_col_block = (proj.shape[1] - pw) // pw
    return pl.pallas_call(
        functools.partial(_mix_kernel, tr=tr, aw=aw, gd=gd),
        grid=(batch, tiles),
        in_specs=[pl.BlockSpec((tr, aw), lambda b, j: (b * tiles + j, 0)),
                  pl.BlockSpec((tr, pw), lambda b, j: (b * tiles + j, u_col_block)),
                  pl.BlockSpec((n_groups, gd, gd), lambda b, j: (0, 0, 0)),
                  pl.BlockSpec((n_groups, gd), lambda b, j: (0, 0)),
                  pl.BlockSpec((1, aw), lambda b, j: (0, 0)),
                  pl.BlockSpec((1, pw), lambda b, j: (0, 0))],
        out_specs=pl.BlockSpec((tr, aw + pw), lambda b, j: (b * tiles + j, 0)),
        out_shape=jax.ShapeDtypeStruct((t, aw + pw), BF16),
        scratch_shapes=[pltpu.VMEM((tr + POOL_HALO, pw), BF16)],
        compiler_params=_params("arbitrary", "arbitrary"),
        name="mixer_norms",
    )(attn, proj, w_pool_bf, pool_scale, g_attn.reshape(1, aw), g_pool.reshape(1, pw))


def _matmul_kernel(a_ref, b_ref, o_ref):
    o_ref[...] = jnp.dot(a_ref[...], b_ref[...], preferred_element_type=F32).astype(o_ref.dtype)


def _matmul(a, b, out_dtype, tm=1024, tn=1024):
    m, k = a.shape
    n = b.shape[1]
    tm, tn = min(tm, m), min(tn, n)
    return pl.pallas_call(
        _matmul_kernel,
        grid=(m // tm, n // tn),
        in_specs=[pl.BlockSpec((tm, k), lambda i, j: (i, 0)),
                  pl.BlockSpec((k, tn), lambda i, j: (0, j))],
        out_specs=pl.BlockSpec((tm, tn), lambda i, j: (i, j)),
        out_shape=jax.ShapeDtypeStruct((m, n), out_dtype),
        compiler_params=_params("arbitrary", "arbitrary"),
        name="out_projection",
    )(a, b)


def _split_bf16(x):
    hi = x.astype(BF16)
    lo = (x - hi.astype(F32)).astype(BF16)
    return hi, lo


def _route(f, w_t, bias):
    f_hi, f_lo = _split_bf16(f)
    w_hi, w_lo = _split_bf16(w_t)
    logits = (lax.dot_general(w_hi, f_hi, _NT, preferred_element_type=F32)
              + lax.dot_general(w_hi, f_lo, _NT, preferred_element_type=F32)
              + lax.dot_general(w_lo, f_hi, _NT, preferred_element_type=F32))
    n_exp, tm = logits.shape
    per = n_exp // N_EXPERT_GROUPS
    scores = jax.nn.sigmoid(logits)
    biased = scores + bias
    ninf = -jnp.inf

    b3 = biased.reshape(N_EXPERT_GROUPS, per, tm)
    memb = lax.broadcasted_iota(I32, b3.shape, 1)
    m1 = jnp.max(b3, axis=1, keepdims=True)
    i1 = jnp.min(jnp.where(b3 == m1, memb, per), axis=1, keepdims=True)
    m2 = jnp.max(jnp.where(memb == i1, ninf, b3), axis=1, keepdims=True)
    gscore = (m1 + m2).reshape(N_EXPERT_GROUPS, tm)

    gid = lax.broadcasted_iota(I32, gscore.shape, 0)
    gmask = jnp.zeros(gscore.shape, jnp.bool_)
    cur = gscore
    for _ in range(TOPK_GROUPS):
        mx = jnp.max(cur, axis=0, keepdims=True)
        ix = jnp.min(jnp.where(cur == mx, gid, N_EXPERT_GROUPS), axis=0, keepdims=True)
        sel = gid == ix
        gmask = gmask | sel
        cur = jnp.where(sel, ninf, cur)
    emask = jnp.broadcast_to(gmask.reshape(N_EXPERT_GROUPS, 1, tm), b3.shape).reshape(n_exp, tm)

    masked = jnp.where(emask, biased, ninf)
    eid = lax.broadcasted_iota(I32, masked.shape, 0)
    idx_rows, w_rows, sels = [], [], []
    chosen = jnp.zeros(masked.shape, F32)
    for _ in range(TOP_K):
        mx = jnp.max(masked, axis=0, keepdims=True)
        ix = jnp.min(jnp.where(masked == mx, eid, n_exp), axis=0, keepdims=True)
        sel = eid == ix
        idx_rows.append(ix)
        w_rows.append(jnp.sum(jnp.where(sel, scores, 0.0), axis=0, keepdims=True))
        sels.append(sel)
        chosen = jnp.where(sel, 1.0, chosen)
        masked = jnp.where(sel, ninf, masked)
    wsel = jnp.concatenate(w_rows, axis=0)
    wts = wsel / jnp.sum(wsel, axis=0, keepdims=True) * ROUTED_SCALE
    return jnp.concatenate(idx_rows, axis=0), wts, sels, chosen


def _postmix_kernel(y_ref, x_ref, mod_ref, gpost_ref, gpre_ref, wt_ref, bias_ref,
                    h_ref, fpk_ref, idx_ref, wts_ref, rank_ref, cum_ref, carry):
    i = pl.program_id(0)

    @pl.when(i == 0)
    def _():
        carry[...] = jnp.zeros(carry.shape, F32)

    gate1 = mod_ref[0, 2:3, :]
    shift2 = mod_ref[0, 3:4, :]
    scale2 = mod_ref[0, 4:5, :]
    h = x_ref[...] + gate1 * _rms(y_ref[...], gpost_ref[...])
    h_ref[...] = h
    f = _rms(h, gpre_ref[...]) * (1.0 + scale2) + shift2
    fpk_ref[...] = _pack_bf16_pair(f)

    idx, wts, sels, chosen = _route(f, wt_ref[...], bias_ref[...])
    tm = f.shape[0]
    upper = (lax.broadcasted_iota(I32, (tm, tm), 0)
             <= lax.broadcasted_iota(I32, (tm, tm), 1)).astype(BF16)
    cum = jnp.dot(chosen.astype(BF16), upper, preferred_element_type=F32) + carry[...]
    carry[...] = cum[:, tm - 1:tm]
    idx_ref[...] = idx
    wts_ref[...] = wts
    cum_ref[...] = cum
    rank_ref[...] = jnp.concatenate(
        [jnp.sum(jnp.where(sel, cum, 0.0), axis=0, keepdims=True) for sel in sels],
        axis=0).astype(I32) - 1


def _post_mix_route(y, x2, mod, g_post, g_pre, w_router, router_bias, seq):
    t, d = x2.shape
    n_exp = w_router.shape[1]
    tm = min(256, seq)
    tiles_per_seq = seq // tm
    row = pl.BlockSpec((tm, d), lambda i: (i, 0))
    vec = pl.BlockSpec((1, d), lambda i: (0, 0))
    kt = pl.BlockSpec((TOP_K, tm), lambda i: (0, i))
    return pl.pallas_call(
        _postmix_kernel,
        grid=(t // tm,),
        in_specs=[row, row,
                  pl.BlockSpec((1, N_MOD, d), lambda i: (i // tiles_per_seq, 0, 0)),
                  vec, vec,
                  pl.BlockSpec((n_exp, d), lambda i: (0, 0)),
                  pl.BlockSpec((n_exp, 1), lambda i: (0, 0))],
        out_specs=[row, pl.BlockSpec((tm, d // 2), lambda i: (i, 0)), kt, kt, kt,
                   pl.BlockSpec((n_exp, tm), lambda i: (0, i))],
        out_shape=[jax.ShapeDtypeStruct((t, d), F32),
                   jax.ShapeDtypeStruct((t, d // 2), U32),
                   jax.ShapeDtypeStruct((TOP_K, t), I32),
                   jax.ShapeDtypeStruct((TOP_K, t), F32),
                   jax.ShapeDtypeStruct((TOP_K, t), I32),
                   jax.ShapeDtypeStruct((n_exp, t), F32)],
        scratch_shapes=[pltpu.VMEM((n_exp, 1), F32)],
        compiler_params=_params("arbitrary"),
        name="post_mix_route",
    )(y, x2, mod, g_post.reshape(1, d), g_pre.reshape(1, d), w_router.T, router_bias.reshape(n_exp, 1))


def _dispatch_tables(idx_t, rank_t, cum, rows_per_block, tokens_per_tile):
    k, t = idx_t.shape
    n_exp = cum.shape[0]
    n = k * t
    n_blocks = n // rows_per_block + n_exp
    experts = jnp.arange(n_exp, dtype=I32)
    counts = cum[:, -1].astype(I32)
    padded = (counts + rows_per_block - 1) // rows_per_block * rows_per_block
    pad_ends = jnp.cumsum(padded)
    pad_starts = pad_ends - padded
    n_used = (pad_ends[-1] // rows_per_block).astype(I32)
    n_tab = n_blocks + EXPERT_LOOKAHEAD
    blk_start = jnp.arange(n_tab, dtype=I32) * rows_per_block
    block_e = jnp.minimum(jnp.sum((pad_ends[None, :] <= blk_start[:, None]).astype(I32), axis=1),
                          n_exp - 1).astype(I32)
    of_block = block_e[:, None] == experts[None, :]
    first_row = blk_start - jnp.sum(jnp.where(of_block, pad_starts[None, :], 0), axis=1)
    prev_e = jnp.concatenate([jnp.full((1,), -1, I32), block_e[:-1]])
    first = (block_e != prev_e).astype(I32)
    after = jnp.sum(jnp.where(of_block, pad_ends[None, :], 0), axis=1) // rows_per_block
    next_e = jnp.where(after < n_used, block_e[jnp.minimum(after, n_tab - 1)], -1).astype(I32)
    dest = jnp.sum(jnp.where(idx_t[:, :, None] == experts[None, None, :],
                             pad_starts[None, None, :], 0), axis=2) + rank_t
    n_tiles = t // tokens_per_tile
    dest_tiles = dest.reshape(k, n_tiles, tokens_per_tile).transpose(1, 0, 2).reshape(
        n_tiles, k * tokens_per_tile).astype(I32)
    return (block_e, first, next_e, first_row.astype(I32), n_used.reshape(1)), dest_tiles


def _expert_kernel(be_ref, first_ref, nexte_ref, row0_ref, nused_ref,
                   cum_hbm, ends_hbm, fpk_hbm, wg_hbm, wu_hbm, wd_hbm,
                   o_ref,
                   cum_v, ends_v, tok_v0, tok_v1, tok_s0, tok_s1, xs0, xs1,
                   stage_g, stage_u, stage_d, wg, wu, wd,
                   sem_cum, sem_tok, sem_x, sem_w, *, rows, n_tokens):
    i = pl.program_id(0)
    n_used = nused_ref[0]
    half = wd.shape[1] // 2
    n_buckets = cum_v.shape[1]
    tok_v, tok_s, xs = (tok_v0, tok_v1), (tok_s0, tok_s1), (xs0, xs1)

    def tok_copy(slot):
        return pltpu.make_async_copy(tok_v[slot], tok_s[slot], sem_tok.at[slot])

    def invert_start(blk, slot):
        e = be_ref[blk]
        row0 = row0_ref[blk]
        ends = ends_v[pl.ds(e, 1), :]
        counts = cum_v[e]
        c_hi = jnp.floor(counts * (1.0 / LANES))
        c_lo = counts - c_hi * LANES
        if n_buckets < LANES:
            fill = jnp.zeros((LANES - n_buckets, LANES), F32)
            c_hi = jnp.concatenate([c_hi, fill], axis=0)
            c_lo = jnp.concatenate([c_lo, fill], axis=0)
        c_hi = c_hi.astype(BF16)
        c_lo = c_lo.astype(BF16)
        ones = jnp.ones((SUBLANES, LANES), BF16)
        lane = lax.broadcasted_iota(I32, (LANES, LANES), 1)
        for c in range(rows // LANES):
            j = (row0 + c * LANES + lax.broadcasted_iota(I32, (LANES, 1), 0)).astype(F32)
            done = jnp.where(ends <= j, 1.0, 0.0)
            before = jnp.where(lane == 0, 1.0, pltpu.roll(done, 1, 1))
            pick = ((1.0 - done) * before).astype(BF16)
            in_bucket = (jnp.dot(pick, c_hi, preferred_element_type=F32) * LANES
                         + jnp.dot(pick, c_lo, preferred_element_type=F32))
            below = jnp.where(in_bucket <= j, 1.0, 0.0).astype(BF16)
            n_done = lax.dot_general(ones, done.astype(BF16), _NT, preferred_element_type=F32)
            n_below = lax.dot_general(ones, below, _NT, preferred_element_type=F32)
            tok = jnp.minimum(n_done * LANES + n_below, n_tokens - 1.0)
            tok_v[slot][:, c * LANES:(c + 1) * LANES] = tok.astype(I32)
        tok_copy(slot).start()

    def gather_row(slot, r):
        tok = tok_s[slot][0, r]
        pltpu.make_async_copy(fpk_hbm.at[pl.ds(tok, 1), :], xs[slot].at[pl.ds(r, 1), :],
                              sem_x.at[slot]).start()

    def gather_start(slot, unrolled):
        if unrolled:
            for r in range(rows):
                gather_row(slot, r)
        else:
            def body(r, c):
                gather_row(slot, r)
                return c
            lax.fori_loop(0, rows, body, 0, unroll=8)

    def gather_wait(slot):
        pltpu.make_async_copy(fpk_hbm.at[pl.ds(0, rows), :], xs[slot], sem_x.at[slot]).wait()

    def weight_pairs(e):
        return ((wg_hbm.at[e], stage_g), (wu_hbm.at[e], stage_u), (wd_hbm.at[e], stage_d))

    def weights_start(e):
        for k, (src, dst) in enumerate(weight_pairs(e)):
            pltpu.async_copy(src, dst, sem_w.at[k], priority=1)

    def weights_wait(e):
        for k, (src, dst) in enumerate(weight_pairs(e)):
            pltpu.make_async_copy(src, dst, sem_w.at[k]).wait()

    def cast_weights(src, dst):
        chunk = math.gcd(256, src.shape[0])
        def body(c, carry):
            r0 = pl.multiple_of(c * chunk, chunk)
            dst[pl.ds(r0, chunk), :] = src[pl.ds(r0, chunk), :].astype(BF16)
            return carry
        lax.fori_loop(0, src.shape[0] // chunk, body, 0)

    @pl.when((i == 0) & (n_used > 0))
    def _():
        cum_copy = pltpu.make_async_copy(cum_hbm, cum_v, sem_cum.at[0])
        ends_copy = pltpu.make_async_copy(ends_hbm, ends_v, sem_cum.at[1])
        cum_copy.start()
        ends_copy.start()
        weights_start(be_ref[0])
        cum_copy.wait()
        ends_copy.wait()
        invert_start(0, 0)
        tok_copy(0).wait()
        gather_start(0, unrolled=False)
        invert_start(1, 1)

    def run_block(cur):
        nxt = 1 - cur
        gather_wait(cur)
        tok_copy(nxt).wait()
        gather_start(nxt, unrolled=True)
        invert_start(i + 2, cur)
        lo, hi = _unpack_bf16_pair(xs[cur][...])
        x_lo = lo.astype(BF16)
        x_hi = hi.astype(BF16)
        g = (jnp.dot(x_lo, wg[0:half, :], preferred_element_type=F32)
             + jnp.dot(x_hi, wg[half:2 * half, :], preferred_element_type=F32))
        u = (jnp.dot(x_lo, wu[0:half, :], preferred_element_type=F32)
             + jnp.dot(x_hi, wu[half:2 * half, :], preferred_element_type=F32))
        act = (_silu(g) * u).astype(BF16)
        o_ref[...] = _pack_bf16_pair(jnp.dot(act, wd[...], preferred_element_type=F32))

        @pl.when(i == n_used - 1)
        def _():
            gather_wait(nxt)
            tok_copy(cur).wait()

    @pl.when(i < n_used)
    def _():
        @pl.when(first_ref[i] == 1)
        def _():
            weights_wait(be_ref[i])
            cast_weights(stage_g, wg)
            cast_weights(stage_u, wu)
            cast_weights(stage_d, wd)

            @pl.when(nexte_ref[i] >= 0)
            def _():
                weights_start(nexte_ref[i])

        @pl.when(i % 2 == 0)
        def _():
            run_block(0)

        @pl.when(i % 2 == 1)
        def _():
            run_block(1)

    @pl.when(i >= n_used)
    def _():
        o_ref[...] = jnp.zeros(o_ref.shape, o_ref.dtype)


def _routed_experts(fpk, cum, block_tables, w_gate, w_up, w_down):
    block_e, first, next_e, first_row, n_used = block_tables
    t, half = fpk.shape
    d = 2 * half
    n_exp, _, de = w_gate.shape
    rows = MOE_ROWS
    n_blocks = block_e.shape[0] - EXPERT_LOOKAHEAD
    n_buckets = t // LANES
    assert n_buckets <= LANES, "two-level token search covers at most 128 buckets of 128 tokens"
    cum3 = cum.reshape(n_exp, n_buckets, LANES)
    ends = jnp.pad(cum3[:, :, LANES - 1], ((0, 0), (0, LANES - n_buckets)),
                   constant_values=float(2 * t))
    any_spec = pl.BlockSpec(memory_space=pl.ANY)
    grid_spec = pltpu.PrefetchScalarGridSpec(
        num_scalar_prefetch=5,
        grid=(n_blocks,),
        in_specs=[any_spec, any_spec, any_spec, any_spec, any_spec, any_spec],
        out_specs=pl.BlockSpec((rows, half), lambda i, *_: (i, 0)),
        scratch_shapes=[pltpu.VMEM((n_exp, n_buckets, LANES), F32),
                        pltpu.VMEM((n_exp, LANES), F32),
                        pltpu.VMEM((SUBLANES, rows), I32),
                        pltpu.VMEM((SUBLANES, rows), I32),
                        pltpu.SMEM((SUBLANES, rows), I32),
                        pltpu.SMEM((SUBLANES, rows), I32),
                        pltpu.VMEM((rows, half), U32),
                        pltpu.VMEM((rows, half), U32),
                        pltpu.VMEM((d, de), F32),
                        pltpu.VMEM((d, de), F32),
                        pltpu.VMEM((de, d), F32),
                        pltpu.VMEM((d, de), BF16),
                        pltpu.VMEM((d, de), BF16),
                        pltpu.VMEM((de, d), BF16),
                        pltpu.SemaphoreType.DMA((2,)),
                        pltpu.SemaphoreType.DMA((2,)),
                        pltpu.SemaphoreType.DMA((2,)),
                        pltpu.SemaphoreType.DMA((3,))])
    return pl.pallas_call(
        functools.partial(_expert_kernel, rows=rows, n_tokens=t),
        grid_spec=grid_spec,
        out_shape=jax.ShapeDtypeStruct((n_blocks * rows, half), U32),
        compiler_params=_params("arbitrary"),
        name="routed_experts",
    )(block_e, first, next_e, first_row, n_used, cum3, ends, fpk, w_gate, w_up, w_down)


def _shared_kernel(fpk_ref, wg_ref, wu_ref, wd_ref, o_ref):
    half = fpk_ref.shape[1]
    lo, hi = _unpack_bf16_pair(fpk_ref[...])
    x_lo = lo.astype(BF16)
    x_hi = hi.astype(BF16)
    g = (jnp.dot(x_lo, wg_ref[0:half, :], preferred_element_type=F32)
         + jnp.dot(x_hi, wg_ref[half:2 * half, :], preferred_element_type=F32))
    u = (jnp.dot(x_lo, wu_ref[0:half, :], preferred_element_type=F32)
         + jnp.dot(x_hi, wu_ref[half:2 * half, :], preferred_element_type=F32))
    act = (_silu(g) * u).astype(BF16)
    o_ref[...] = jnp.dot(act, wd_ref[...], preferred_element_type=F32).astype(o_ref.dtype)


def _shared_expert(fpk, wg_bf, wu_bf, wd_bf):
    t, half = fpk.shape
    d = 2 * half
    ds = wg_bf.shape[1]
    tm = min(512, t)
    return pl.pallas_call(
        _shared_kernel,
        grid=(t // tm,),
        in_specs=[pl.BlockSpec((tm, half), lambda i: (i, 0)),
                  pl.BlockSpec((d, ds), lambda i: (0, 0)),
                  pl.BlockSpec((d, ds), lambda i: (0, 0)),
                  pl.BlockSpec((ds, d), lambda i: (0, 0))],
        out_specs=pl.BlockSpec((tm, d), lambda i: (i, 0)),
        out_shape=jax.ShapeDtypeStruct((t, d), BF16),
        compiler_params=_params("arbitrary"),
        name="shared_expert",
    )(fpk, wg_bf, wu_bf, wd_bf)


def _combine_kernel(dest_hbm, ys_hbm, wts_ref, sh_ref, h_ref, mod_ref, g_ref, o_ref,
                    dest_s0, dest_s1, gbuf0, gbuf1, sem_d, sem_g, *, tm, top_k, n_tiles):
    i = pl.program_id(0)
    n_rows = top_k * tm
    half = gbuf0.shape[1]
    dest_s, gbuf = (dest_s0, dest_s1), (gbuf0, gbuf1)
    last = n_tiles - 1

    def dest_copy(tile, s):
        return pltpu.make_async_copy(dest_hbm.at[tile], dest_s[s], sem_d.at[s])

    def gather_row(s, r, priority):
        row = dest_s[s][r]
        pltpu.async_copy(ys_hbm.at[pl.ds(row, 1), :], gbuf[s].at[pl.ds(r, 1), :], sem_g.at[s],
                         priority=priority)

    def gather_start(s, unrolled):
        if unrolled:
            for r in range(n_rows):
                gather_row(s, r, r % 2)
        else:
            def body(r, c):
                gather_row(s, r, 0)
                return c
            lax.fori_loop(0, n_rows, body, 0, unroll=8)

    def gather_wait(s):
        pltpu.make_async_copy(ys_hbm.at[pl.ds(0, n_rows), :], gbuf[s], sem_g.at[s]).wait()

    @pl.when(i == 0)
    def _():
        first_dest = dest_copy(0, 0)
        first_dest.start()
        first_dest.wait()
        gather_start(0, unrolled=False)
        dest_copy(jnp.minimum(1, last), 1).start()

    def run_tile(cur):
        nxt = 1 - cur
        gather_wait(cur)
        dest_copy(jnp.minimum(i + 1, last), nxt).wait()
        gather_start(nxt, unrolled=True)
        dest_copy(jnp.minimum(i + 2, last), cur).start()
        acc_lo = jnp.zeros((tm, half), F32)
        acc_hi = jnp.zeros((tm, half), F32)
        for k in range(top_k):
            lo, hi = _unpack_bf16_pair(gbuf[cur][k * tm:(k + 1) * tm, :])
            w = wts_ref[:, k:k + 1]
            acc_lo = acc_lo + w * lo
            acc_hi = acc_hi + w * hi
        moe_lo = acc_lo + sh_ref[:, 0:half].astype(F32)
        moe_hi = acc_hi + sh_ref[:, half:2 * half].astype(F32)
        ssq = (jnp.sum(moe_lo * moe_lo, axis=1, keepdims=True)
               + jnp.sum(moe_hi * moe_hi, axis=1, keepdims=True))
        r = lax.rsqrt(ssq * (1.0 / (2 * half)) + NORM_EPS)
        gate2 = mod_ref[0, 5:6, :]
        g = g_ref[...]
        o_ref[:, 0:half] = h_ref[:, 0:half] + gate2[:, 0:half] * (moe_lo * r * g[:, 0:half])
        o_ref[:, half:2 * half] = (h_ref[:, half:2 * half]
                                   + gate2[:, half:2 * half] * (moe_hi * r * g[:, half:2 * half]))

        @pl.when(i == last)
        def _():
            gather_wait(nxt)
            dest_copy(last, cur).wait()

    @pl.when(i % 2 == 0)
    def _():
        run_tile(0)

    @pl.when(i % 2 == 1)
    def _():
        run_tile(1)


def _combine(ys, dest_tiles, wts, shared, h, mod, g_post, seq, top_k):
    t, d = h.shape
    tm = COMBINE_TOKENS
    n_tiles = t // tm
    tiles_per_seq = seq // tm
    row = pl.BlockSpec((tm, d), lambda i: (i, 0))
    any_spec = pl.BlockSpec(memory_space=pl.ANY)
    return pl.pallas_call(
        functools.partial(_combine_kernel, tm=tm, top_k=top_k, n_tiles=n_tiles),
        grid=(n_tiles,),
        in_specs=[any_spec, any_spec,
                  pl.BlockSpec((tm, top_k), lambda i: (i, 0)),
                  row, row,
                  pl.BlockSpec((1, N_MOD, d), lambda i: (i // tiles_per_seq, 0, 0)),
                  pl.BlockSpec((1, d), lambda i: (0, 0))],
        out_specs=row,
        out_shape=jax.ShapeDtypeStruct((t, d), F32),
        scratch_shapes=[pltpu.SMEM((top_k * tm,), I32),
                        pltpu.SMEM((top_k * tm,), I32),
                        pltpu.VMEM((top_k * tm, d // 2), U32),
                        pltpu.VMEM((top_k * tm, d // 2), U32),
                        pltpu.SemaphoreType.DMA((2,)),
                        pltpu.SemaphoreType.DMA((2,))],
        compiler_params=_params("arbitrary"),
        name="combine",
    )(dest_tiles, ys, wts, shared, h, mod, g_post.reshape(1, d))


def _rope_tables(seq):
    half = HEAD_DIM // 2
    inv_freq = jnp.exp(-math.log(ROPE_THETA) * jnp.arange(half, dtype=F32) / half)
    ang = jnp.arange(seq, dtype=F32)[:, None] * inv_freq[None, :]
    cos, sin = jnp.cos(ang), jnp.sin(ang)
    return jnp.concatenate([cos, cos], axis=1), jnp.concatenate([-sin, sin], axis=1)


def kernel(x, c, w_mod, b_mod, g_pre_mix, g_post_mix, g_pre_ffn, g_post_ffn, w_in, w_pool, pool_scale, g_attn_out, g_pool_out, w_out, w_router, router_bias, w_gate, w_up, w_down, w_shared_gate, w_shared_up, w_shared_down):
    batch, seq, d = x.shape
    depth = w_mod.shape[0]
    n_groups, gd = w_pool.shape[1], w_pool.shape[2]
    pool_width = n_groups * gd
    attn_width = (w_in.shape[2] - pool_width) // 3
    n_heads = attn_width // HEAD_DIM
    assert n_groups == len(POOL_WINDOWS) and seq % (ATTN_BLOCK * DILATIONS[-1]) == 0
    cosf, sinf = _rope_tables(seq)

    h = x.reshape(batch * seq, d)
    for layer in range(depth):
        mod = _modulation(c, w_mod[layer], b_mod[layer])
        proj = _in_projection(h, g_pre_mix[layer], mod, w_in[layer].astype(BF16), cosf, sinf,
                              seq, attn_width)
        attn = _dilated_attention(proj, batch, seq, n_heads)
        mixed = _mixer_norms(attn, proj, w_pool[layer].astype(BF16), pool_scale[layer],
                             g_attn_out[layer], g_pool_out[layer], batch, seq)
        y = _matmul(mixed, w_out[layer].astype(BF16), F32)
        h, fpk, idx_t, wts_t, rank_t, cum = _post_mix_route(
            y, h, mod, g_post_mix[layer], g_pre_ffn[layer], w_router[layer], router_bias[layer], seq)
        block_tables, dest_tiles = _dispatch_tables(idx_t, rank_t, cum, MOE_ROWS, COMBINE_TOKENS)
        ys = _routed_experts(fpk, cum, block_tables, w_gate[layer], w_up[layer], w_down[layer])
        shared = _shared_expert(fpk, w_shared_gate[layer].astype(BF16),
                                w_shared_up[layer].astype(BF16), w_shared_down[layer].astype(BF16))
        h = _combine(ys, dest_tiles, wts_t.T, shared, h, mod, g_post_ffn[layer], seq, TOP_K)
    return h.reshape(batch, seq, d)
```

```python
import functools
import math

import jax
import jax.numpy as jnp
from jax import lax
from jax.experimental import pallas as pl
from jax.experimental.pallas import tpu as pltpu

F32 = jnp.float32
BF16 = jnp.bfloat16
I32 = jnp.int32
U32 = jnp.uint32

HEAD_DIM = 128
ATTN_BLOCK = 128
DILATIONS = (1, 4, 16)
POOL_WINDOWS = (2, 4, 8, 16)
POOL_HALO = 16
ROPE_THETA = 10000.0
TOP_K = 8
N_EXPERT_GROUPS = 8
TOPK_GROUPS = 4
ROUTED_SCALE = 2.5
N_MOD = 6
NORM_EPS = 1e-6
MASK_VALUE = -1e30

LANES = 128
SUBLANES = 8
VMEM_LIMIT_BYTES = 60 * 1024 * 1024

MOE_ROWS = 256
EXPERT_LOOKAHEAD = 2
EXPERT_ISSUE_GROUPS = 8
COMBINE_TOKENS = 128

_NT = (((1,), (1,)), ((), ()))


def _params(*sem):
    return pltpu.CompilerParams(dimension_semantics=sem, vmem_limit_bytes=VMEM_LIMIT_BYTES)


def _rms(x, g):
    return x * lax.rsqrt(jnp.mean(x * x, axis=-1, keepdims=True) + NORM_EPS) * g


def _silu(x):
    return x * jax.nn.sigmoid(x)


def _pack_bf16_pair(x):
    n = x.shape[1] // 2
    lo = lax.bitcast_convert_type(x[:, :n].astype(BF16).astype(F32), U32)
    hi = lax.bitcast_convert_type(x[:, n:].astype(BF16).astype(F32), U32)
    return (lo >> 16) | hi


def _unpack_bf16_pair(w):
    lo = lax.bitcast_convert_type(w << 16, F32)
    hi = lax.bitcast_convert_type(w & jnp.uint32(0xFFFF0000), F32)
    return lo, hi


def _mod_kernel(c_ref, w_ref, b_ref, o_ref):
    cond = _silu(c_ref[...])
    o_ref[...] = jnp.dot(cond.astype(BF16), w_ref[...].astype(BF16),
                         preferred_element_type=F32) + b_ref[...]


def _modulation(c, w_mod, b_mod):
    b, d = c.shape
    n = w_mod.shape[1]
    bp = -(-b // SUBLANES) * SUBLANES
    tn = 512
    out = pl.pallas_call(
        _mod_kernel,
        grid=(n // tn,),
        in_specs=[pl.BlockSpec((bp, d), lambda j: (0, 0)),
                  pl.BlockSpec((d, tn), lambda j: (0, j)),
                  pl.BlockSpec((1, tn), lambda j: (0, j))],
        out_specs=pl.BlockSpec((bp, tn), lambda j: (0, j)),
        out_shape=jax.ShapeDtypeStruct((bp, n), F32),
        compiler_params=_params("arbitrary"),
        name="modulation",
    )(jnp.pad(c, ((0, bp - b), (0, 0))), w_mod, b_mod.reshape(1, n))
    return out[:b].reshape(b, N_MOD, d)


def _inproj_kernel(x_ref, g_ref, mod_ref, w_ref, cos_ref, sin_ref, o_ref, a_scr, *,
                   n_q_tiles, n_rope_tiles):
    j = pl.program_id(1)

    @pl.when(j == 0)
    def _():
        shift1 = mod_ref[0, 0:1, :]
        scale1 = mod_ref[0, 1:2, :]
        a = _rms(x_ref[...], g_ref[...]) * (1.0 + scale1) + shift1
        a_scr[...] = a.astype(BF16)

    acc = jnp.dot(a_scr[...], w_ref[...], preferred_element_type=F32)

    @pl.when(j < n_rope_tiles)
    def _():
        qk_scale = jnp.where(j < n_q_tiles, HEAD_DIM ** -0.5, 1.0)
        cosf = cos_ref[...] * qk_scale
        sinf = sin_ref[...] * qk_scale
        for hh in range(acc.shape[1] // HEAD_DIM):
            blk = acc[:, hh * HEAD_DIM:(hh + 1) * HEAD_DIM]
            rot = blk * cosf + pltpu.roll(blk, HEAD_DIM // 2, 1) * sinf
            o_ref[:, hh * HEAD_DIM:(hh + 1) * HEAD_DIM] = rot.astype(o_ref.dtype)

    @pl.when(j >= n_rope_tiles)
    def _():
        o_ref[...] = acc.astype(o_ref.dtype)


def _in_projection(x2, g, mod, w_in_bf, cosf, sinf, seq, attn_width):
    t, d = x2.shape
    n = w_in_bf.shape[1]
    tm = min(512, seq)
    tn = min(1024, attn_width)
    tiles_per_seq = seq // tm
    return pl.pallas_call(
        functools.partial(_inproj_kernel, n_q_tiles=attn_width // tn, n_rope_tiles=2 * attn_width // tn),
        grid=(t // tm, n // tn),
        in_specs=[pl.BlockSpec((tm, d), lambda i, j: (i, 0)),
                  pl.BlockSpec((1, d), lambda i, j: (0, 0)),
                  pl.BlockSpec((1, N_MOD, d), lambda i, j: (i // tiles_per_seq, 0, 0)),
                  pl.BlockSpec((d, tn), lambda i, j: (0, j)),
                  pl.BlockSpec((tm, HEAD_DIM), lambda i, j: (i % tiles_per_seq, 0)),
                  pl.BlockSpec((tm, HEAD_DIM), lambda i, j: (i % tiles_per_seq, 0))],
        out_specs=pl.BlockSpec((tm, tn), lambda i, j: (i, j)),
        out_shape=jax.ShapeDtypeStruct((t, n), BF16),
        scratch_shapes=[pltpu.VMEM((tm, d), BF16)],
        compiler_params=_params("arbitrary", "arbitrary"),
        name="in_projection",
    )(x2, g.reshape(1, d), mod, w_in_bf, cosf, sinf)


def _attn_kernel(q_ref, k_ref, v_ref, o_ref, qf, kf, vf, qs, ks, vs, accs, maxs, dens, stage, *, seq):
    blk = ATTN_BLOCK
    n_blk = seq // blk
    qf[...] = q_ref[...].astype(F32)
    kf[...] = k_ref[...].astype(F32)
    vf[...] = v_ref[...].astype(F32)
    vs[:, HEAD_DIM:2 * HEAD_DIM] = jnp.ones((vs.shape[0], HEAD_DIM), BF16)

    qi = lax.broadcasted_iota(I32, (blk, blk), 0)
    ki = lax.broadcasted_iota(I32, (blk, blk), 1)
    causal_bias = jnp.where(ki <= qi, 0.0, MASK_VALUE)
    upper_bias = jnp.where(ki >= qi, 0.0, MASK_VALUE)

    for bi, dil in enumerate(DILATIONS):
        sub_len = seq // dil
        nbk = sub_len // blk
        lead = blk if nbk > 1 else 0
        pitch = sub_len + lead
        def to_sub_order(res, carry, dil=dil, sub_len=sub_len, lead=lead, pitch=pitch):
            base = pl.multiple_of(jnp.asarray(res * pitch, I32), blk)
            q0 = pl.multiple_of(jnp.asarray(res * sub_len, I32), blk)
            if dil == 1:
                q_src, k_src, v_src = q_ref[...], k_ref[...], v_ref[...]
            else:
                q_src = qf[pl.ds(res, sub_len, stride=dil), :].astype(BF16)
                k_src = kf[pl.ds(res, sub_len, stride=dil), :].astype(BF16)
                v_src = vf[pl.ds(res, sub_len, stride=dil), :].astype(BF16)
            qs[pl.ds(q0, sub_len), :] = q_src
            if lead:
                ks[pl.ds(base, lead), :] = jnp.zeros((lead, HEAD_DIM), BF16)
                vs[pl.ds(base, lead), 0:HEAD_DIM] = jnp.zeros((lead, HEAD_DIM), BF16)
            ks[pl.ds(base + lead, sub_len), :] = k_src
            vs[pl.ds(base + lead, sub_len), 0:HEAD_DIM] = v_src
            return carry

        if dil == 1:
            to_sub_order(0, 0)
        else:
            lax.fori_loop(0, dil, to_sub_order, 0)

        def blocks(ref, offset, dil=dil, pitch=pitch, sub_len=sub_len, nbk=nbk):
            parts = [ref[res * pitch + offset:res * pitch + offset + sub_len, :]
                     .reshape(nbk, blk, ref.shape[1]) for res in range(dil)]
            return parts[0] if dil == 1 else jnp.concatenate(parts, axis=0)

        q3 = qs[...].reshape(n_blk, blk, HEAD_DIM)
        s_own = jnp.einsum('nqc,nkc->nqk', q3, blocks(ks, lead),
                           preferred_element_type=F32) + causal_bias[None]
        if nbk > 1:
            first = (lax.broadcasted_iota(I32, (n_blk, blk, blk), 0) & (nbk - 1)) == 0
            prev_bias = jnp.where(first, MASK_VALUE, upper_bias[None])
            s_prev = jnp.einsum('nqc,nkc->nqk', q3, blocks(ks, 0),
                                preferred_element_type=F32) + prev_bias
            m = jnp.max(jnp.maximum(s_own, s_prev), axis=-1, keepdims=True)
            accx = (jnp.einsum('nqk,nkc->nqc', jnp.exp(s_own - m).astype(BF16), blocks(vs, lead),
                               preferred_element_type=F32)
                    + jnp.einsum('nqk,nkc->nqc', jnp.exp(s_prev - m).astype(BF16), blocks(vs, 0),
                                 preferred_element_type=F32))
        else:
            m = jnp.max(s_own, axis=-1, keepdims=True)
            accx = jnp.einsum('nqk,nkc->nqc', jnp.exp(s_own - m).astype(BF16), blocks(vs, lead),
                              preferred_element_type=F32)
        m_b = jnp.broadcast_to(m, (n_blk, blk, HEAD_DIM))
        results = (accx[:, :, 0:HEAD_DIM], accx[:, :, HEAD_DIM:2 * HEAD_DIM], m_b)
        if dil == 1:
            for dst, val in zip((accs, dens, maxs), results):
                dst[bi] = val.reshape(seq, HEAD_DIM)
        else:
            for j, val in enumerate(results):
                stage[j] = val.reshape(seq, HEAD_DIM)

            def to_natural_order(res, carry, dil=dil, sub_len=sub_len, bi=bi):
                r0 = pl.multiple_of(jnp.asarray(res * sub_len, I32), blk)
                for j, dst in enumerate((accs, dens, maxs)):
                    dst[bi, pl.ds(res, sub_len, stride=dil), :] = stage[j, pl.ds(r0, sub_len), :]
                return carry

            lax.fori_loop(0, dil, to_natural_order, 0)

    def merge(c, carry):
        r0 = pl.multiple_of(c * blk, blk)
        ms = [maxs[bi, pl.ds(r0, blk), :] for bi in range(len(DILATIONS))]
        m_all = functools.reduce(jnp.maximum, ms)
        ws = [jnp.exp(m - m_all) for m in ms]
        den = sum(w * dens[bi, pl.ds(r0, blk), :] for bi, w in enumerate(ws))
        num = sum(w * accs[bi, pl.ds(r0, blk), :] for bi, w in enumerate(ws))
        o_ref[pl.ds(r0, blk), :] = (num / den).astype(o_ref.dtype)
        return carry

    lax.fori_loop(0, n_blk, merge, 0, unroll=2)


def _dilated_attention(proj, batch, seq, n_heads):
    t = proj.shape[0]
    nb = len(DILATIONS)
    key_rows = max(d * (seq // d + (ATTN_BLOCK if seq // d > ATTN_BLOCK else 0)) for d in DILATIONS)
    return pl.pallas_call(
        functools.partial(_attn_kernel, seq=seq),
        grid=(batch, n_heads),
        in_specs=[pl.BlockSpec((seq, HEAD_DIM), lambda b, h: (b, h)),
                  pl.BlockSpec((seq, HEAD_DIM), lambda b, h: (b, n_heads + h)),
                  pl.BlockSpec((seq, HEAD_DIM), lambda b, h: (b, 2 * n_heads + h))],
        out_specs=pl.BlockSpec((seq, HEAD_DIM), lambda b, h: (b, h)),
        out_shape=jax.ShapeDtypeStruct((t, n_heads * HEAD_DIM), BF16),
        scratch_shapes=[pltpu.VMEM((seq, HEAD_DIM), F32),
                        pltpu.VMEM((seq, HEAD_DIM), F32),
                        pltpu.VMEM((seq, HEAD_DIM), F32),
                        pltpu.VMEM((seq, HEAD_DIM), BF16),
                        pltpu.VMEM((key_rows, HEAD_DIM), BF16),
                        pltpu.VMEM((key_rows, 2 * HEAD_DIM), BF16),
                        pltpu.VMEM((nb, seq, HEAD_DIM), F32),
                        pltpu.VMEM((nb, seq, HEAD_DIM), F32),
                        pltpu.VMEM((nb, seq, HEAD_DIM), F32),
                        pltpu.VMEM((3, seq, HEAD_DIM), F32)],
        compiler_params=_params("arbitrary", "arbitrary"),
        name="dilated_attention",
    )(proj, proj, proj)


def _mix_kernel(attn_ref, u_ref, wp_ref, ps_ref, ga_ref, gp_ref, o_ref, ext, *, tr, aw, gd):
    j = pl.program_id(1)
    n_groups = len(POOL_WINDOWS)

    @pl.when(j == 0)
    def _():
        ext[0:POOL_HALO, :] = jnp.zeros((POOL_HALO, ext.shape[1]), BF16)

    @pl.when(j > 0)
    def _():
        ext[0:POOL_HALO, :] = ext[tr:tr + POOL_HALO, :]

    ext[POOL_HALO:POOL_HALO + tr, :] = u_ref[...]

    o_ref[:, 0:aw] = _rms(attn_ref[...].astype(F32), ga_ref[...]).astype(o_ref.dtype)

    pos = j * tr + lax.broadcasted_iota(I32, (tr, 1), 0)
    diff = (lax.broadcasted_iota(I32, (tr, tr + POOL_HALO), 0) + POOL_HALO
            - lax.broadcasted_iota(I32, (tr, tr + POOL_HALO), 1))
    ys = []
    ssq = jnp.zeros((tr, 1), F32)
    for g, w in enumerate(POOL_WINDOWS):
        band = ((diff >= 0) & (diff < w)).astype(BF16)
        win = jnp.dot(band, ext[:, g * gd:(g + 1) * gd], preferred_element_type=F32)
        cnt = jnp.minimum(pos + 1, w).astype(F32)
        mix = win * (1.0 / cnt) - u_ref[:, g * gd:(g + 1) * gd].astype(F32)
        y = jnp.dot(mix.astype(BF16), wp_ref[g], preferred_element_type=F32) * ps_ref[g:g + 1, :]
        ys.append(y)
        ssq = ssq + jnp.sum(y * y, axis=1, keepdims=True)
    r = lax.rsqrt(ssq * (1.0 / (n_groups * gd)) + NORM_EPS)
    for g in range(n_groups):
        o_ref[:, aw + g * gd:aw + (g + 1) * gd] = (
            ys[g] * r * gp_ref[:, g * gd:(g + 1) * gd]).astype(o_ref.dtype)


def _mixer_norms(attn, proj, w_pool_bf, pool_scale, g_attn, g_pool, batch, seq):
    t, aw = attn.shape
    n_groups, gd, _ = w_pool_bf.shape
    pw = n_groups * gd
    tr = min(256, seq)
    tiles = seq // tr
    u_col_block = (proj.shape[1] - pw) // pw
    return pl.pallas_call(
        functools.partial(_mix_kernel, tr=tr, aw=aw, gd=gd),
        grid=(batch, tiles),
        in_specs=[pl.BlockSpec((tr, aw), lambda b, j: (b * tiles + j, 0)),
                  pl.BlockSpec((tr, pw), lambda b, j: (b * tiles + j, u_col_block)),
                  pl.BlockSpec((n_groups, gd, gd), lambda b, j: (0, 0, 0)),
                  pl.BlockSpec((n_groups, gd), lambda b, j: (0, 0)),
                  pl.BlockSpec((1, aw), lambda b, j: (0, 0)),
                  pl.BlockSpec((1, pw), lambda b, j: (0, 0))],
        out_specs=pl.BlockSpec((tr, aw + pw), lambda b, j: (b * tiles + j, 0)),
        out_shape=jax.ShapeDtypeStruct((t, aw + pw), BF16),
        scratch_shapes=[pltpu.VMEM((tr + POOL_HALO, pw), BF16)],
        compiler_params=_params("arbitrary", "arbitrary"),
        name="mixer_norms",
    )(attn, proj, w_pool_bf, pool_scale, g_attn.reshape(1, aw), g_pool.reshape(1, pw))


def _matmul_kernel(a_ref, b_ref, o_ref):
    o_ref[...] = jnp.dot(a_ref[...], b_ref[...], preferred_element_type=F32).astype(o_ref.dtype)


def _matmul(a, b, out_dtype, tm=1024, tn=1024):
    m, k = a.shape
    n = b.shape[1]
    tm, tn = min(tm, m), min(tn, n)
    return pl.pallas_call(
        _matmul_kernel,
        grid=(m // tm, n // tn),
        in_specs=[pl.BlockSpec((tm, k), lambda i, j: (i, 0)),
                  pl.BlockSpec((k, tn), lambda i, j: (0, j))],
        out_specs=pl.BlockSpec((tm, tn), lambda i, j: (i, j)),
        out_shape=jax.ShapeDtypeStruct((m, n), out_dtype),
        compiler_params=_params("arbitrary", "arbitrary"),
        name="out_projection",
    )(a, b)


def _split_bf16(x):
    hi = x.astype(BF16)
    lo = (x - hi.astype(F32)).astype(BF16)
    return hi, lo


def _route(f, w_t, bias):
    f_hi, f_lo = _split_bf16(f)
    w_hi, w_lo = _split_bf16(w_t)
    logits = (lax.dot_general(w_hi, f_hi, _NT, preferred_element_type=F32)
              + lax.dot_general(w_hi, f_lo, _NT, preferred_element_type=F32)
              + lax.dot_general(w_lo, f_hi, _NT, preferred_element_type=F32))
    n_exp, tm = logits.shape
    per = n_exp // N_EXPERT_GROUPS
    scores = jax.nn.sigmoid(logits)
    biased = scores + bias
    ninf = -jnp.inf

    b3 = biased.reshape(N_EXPERT_GROUPS, per, tm)
    memb = lax.broadcasted_iota(I32, b3.shape, 1)
    m1 = jnp.max(b3, axis=1, keepdims=True)
    i1 = jnp.min(jnp.where(b3 == m1, memb, per), axis=1, keepdims=True)
    m2 = jnp.max(jnp.where(memb == i1, ninf, b3), axis=1, keepdims=True)
    gscore = (m1 + m2).reshape(N_EXPERT_GROUPS, tm)

    gid = lax.broadcasted_iota(I32, gscore.shape, 0)
    gmask = jnp.zeros(gscore.shape, jnp.bool_)
    cur = gscore
    for _ in range(TOPK_GROUPS):
        mx = jnp.max(cur, axis=0, keepdims=True)
        ix = jnp.min(jnp.where(cur == mx, gid, N_EXPERT_GROUPS), axis=0, keepdims=True)
        sel = gid == ix
        gmask = gmask | sel
        cur = jnp.where(sel, ninf, cur)
    emask = jnp.broadcast_to(gmask.reshape(N_EXPERT_GROUPS, 1, tm), b3.shape).reshape(n_exp, tm)

    masked = jnp.where(emask, biased, ninf)
    eid = lax.broadcasted_iota(I32, masked.shape, 0)
    idx_rows, w_rows, sels = [], [], []
    chosen = jnp.zeros(masked.shape, F32)
    for _ in range(TOP_K):
        mx = jnp.max(masked, axis=0, keepdims=True)
        ix = jnp.min(jnp.where(masked == mx, eid, n_exp), axis=0, keepdims=True)
        sel = eid == ix
        idx_rows.append(ix)
        w_rows.append(jnp.sum(jnp.where(sel, scores, 0.0), axis=0, keepdims=True))
        sels.append(sel)
        chosen = jnp.where(sel, 1.0, chosen)
        masked = jnp.where(sel, ninf, masked)
    wsel = jnp.concatenate(w_rows, axis=0)
    wts = wsel / jnp.sum(wsel, axis=0, keepdims=True) * ROUTED_SCALE
    return jnp.concatenate(idx_rows, axis=0), wts, sels, chosen


def _postmix_kernel(y_ref, x_ref, mod_ref, gpost_ref, gpre_ref, wt_ref, bias_ref,
                    h_ref, fpk_ref, idx_ref, wts_ref, rank_ref, cum_ref, carry):
    i = pl.program_id(0)

    @pl.when(i == 0)
    def _():
        carry[...] = jnp.zeros(carry.shape, F32)

    gate1 = mod_ref[0, 2:3, :]
    shift2 = mod_ref[0, 3:4, :]
    scale2 = mod_ref[0, 4:5, :]
    h = x_ref[...] + gate1 * _rms(y_ref[...], gpost_ref[...])
    h_ref[...] = h
    f = _rms(h, gpre_ref[...]) * (1.0 + scale2) + shift2
    fpk_ref[...] = _pack_bf16_pair(f)

    idx, wts, sels, chosen = _route(f, wt_ref[...], bias_ref[...])
    tm = f.shape[0]
    upper = (lax.broadcasted_iota(I32, (tm, tm), 0)
             <= lax.broadcasted_iota(I32, (tm, tm), 1)).astype(BF16)
    cum = jnp.dot(chosen.astype(BF16), upper, preferred_element_type=F32) + carry[...]
    carry[...] = cum[:, tm - 1:tm]
    idx_ref[...] = idx
    wts_ref[...] = wts
    cum_ref[...] = cum
    rank_ref[...] = jnp.concatenate(
        [jnp.sum(jnp.where(sel, cum, 0.0), axis=0, keepdims=True) for sel in sels],
        axis=0).astype(I32) - 1


def _post_mix_route(y, x2, mod, g_post, g_pre, w_router, router_bias, seq):
    t, d = x2.shape
    n_exp = w_router.shape[1]
    tm = min(256, seq)
    tiles_per_seq = seq // tm
    row = pl.BlockSpec((tm, d), lambda i: (i, 0))
    vec = pl.BlockSpec((1, d), lambda i: (0, 0))
    kt = pl.BlockSpec((TOP_K, tm), lambda i: (0, i))
    return pl.pallas_call(
        _postmix_kernel,
        grid=(t // tm,),
        in_specs=[row, row,
                  pl.BlockSpec((1, N_MOD, d), lambda i: (i // tiles_per_seq, 0, 0)),
                  vec, vec,
                  pl.BlockSpec((n_exp, d), lambda i: (0, 0)),
                  pl.BlockSpec((n_exp, 1), lambda i: (0, 0))],
        out_specs=[row, pl.BlockSpec((tm, d // 2), lambda i: (i, 0)), kt, kt, kt,
                   pl.BlockSpec((n_exp, tm), lambda i: (0, i))],
        out_shape=[jax.ShapeDtypeStruct((t, d), F32),
                   jax.ShapeDtypeStruct((t, d // 2), U32),
                   jax.ShapeDtypeStruct((TOP_K, t), I32),
                   jax.ShapeDtypeStruct((TOP_K, t), F32),
                   jax.ShapeDtypeStruct((TOP_K, t), I32),
                   jax.ShapeDtypeStruct((n_exp, t), F32)],
        scratch_shapes=[pltpu.VMEM((n_exp, 1), F32)],
        compiler_params=_params("arbitrary"),
        name="post_mix_route",
    )(y, x2, mod, g_post.reshape(1, d), g_pre.reshape(1, d), w_router.T, router_bias.reshape(n_exp, 1))


def _dispatch_tables(idx_t, rank_t, cum, rows_per_block, tokens_per_tile):
    k, t = idx_t.shape
    n_exp = cum.shape[0]
    n = k * t
    n_blocks = n // rows_per_block + n_exp
    experts = jnp.arange(n_exp, dtype=I32)
    counts = cum[:, -1].astype(I32)
    padded = (counts + rows_per_block - 1) // rows_per_block * rows_per_block
    pad_ends = jnp.cumsum(padded)
    pad_starts = pad_ends - padded
    n_used = (pad_ends[-1] // rows_per_block).astype(I32)
    n_tab = n_blocks + EXPERT_LOOKAHEAD
    blk_start = jnp.arange(n_tab, dtype=I32) * rows_per_block
    block_e = jnp.minimum(jnp.sum((pad_ends[None, :] <= blk_start[:, None]).astype(I32), axis=1),
                          n_exp - 1).astype(I32)
    of_block = block_e[:, None] == experts[None, :]
    first_row = blk_start - jnp.sum(jnp.where(of_block, pad_starts[None, :], 0), axis=1)
    prev_e = jnp.concatenate([jnp.full((1,), -1, I32), block_e[:-1]])
    first = (block_e != prev_e).astype(I32)
    after = jnp.sum(jnp.where(of_block, pad_ends[None, :], 0), axis=1) // rows_per_block
    next_e = jnp.where(after < n_used, block_e[jnp.minimum(after, n_tab - 1)], -1).astype(I32)
    dest = jnp.sum(jnp.where(idx_t[:, :, None] == experts[None, None, :],
                             pad_starts[None, None, :], 0), axis=2) + rank_t
    n_tiles = t // tokens_per_tile
    dest_tiles = dest.reshape(k, n_tiles, tokens_per_tile).transpose(1, 0, 2).reshape(
        n_tiles, k * tokens_per_tile).astype(I32)
    return (block_e, first, next_e, first_row.astype(I32), n_used.reshape(1)), dest_tiles


def _expert_kernel(be_ref, first_ref, nexte_ref, row0_ref, nused_ref,
                   cum_hbm, ends_hbm, fpk_hbm, wg_hbm, wu_hbm, wd_hbm,
                   o_ref,
                   cum_v, ends_v, tok_v0, tok_v1, tok_s0, tok_s1, xs0, xs1,
                   stage_g, stage_u, stage_d, wg, wu, wd,
                   sem_cum, sem_tok, sem_x, sem_w, sem_fence, *, rows, n_tokens):
    i = pl.program_id(0)
    n_used = nused_ref[0]
    half = wd.shape[1] // 2
    n_buckets = cum_v.shape[1]
    tok_v, tok_s, xs = (tok_v0, tok_v1), (tok_s0, tok_s1), (xs0, xs1)

    def tok_copy(slot):
        return pltpu.make_async_copy(tok_v[slot], tok_s[slot], sem_tok.at[slot])

    def invert_start(blk, slot):
        e = be_ref[blk]
        row0 = row0_ref[blk]
        ends = ends_v[pl.ds(e, 1), :]
        counts = cum_v[e]
        c_hi = jnp.floor(counts * (1.0 / LANES))
        c_lo = counts - c_hi * LANES
        if n_buckets < LANES:
            fill = jnp.zeros((LANES - n_buckets, LANES), F32)
            c_hi = jnp.concatenate([c_hi, fill], axis=0)
            c_lo = jnp.concatenate([c_lo, fill], axis=0)
        c_hi = c_hi.astype(BF16)
        c_lo = c_lo.astype(BF16)
        ones = jnp.ones((SUBLANES, LANES), BF16)
        lane = lax.broadcasted_iota(I32, (LANES, LANES), 1)
        for c in range(rows // LANES):
            j = (row0 + c * LANES + lax.broadcasted_iota(I32, (LANES, 1), 0)).astype(F32)
            done = jnp.where(ends <= j, 1.0, 0.0)
            before = jnp.where(lane == 0, 1.0, pltpu.roll(done, 1, 1))
            pick = ((1.0 - done) * before).astype(BF16)
            in_bucket = (jnp.dot(pick, c_hi, preferred_element_type=F32) * LANES
                         + jnp.dot(pick, c_lo, preferred_element_type=F32))
            below = jnp.where(in_bucket <= j, 1.0, 0.0).astype(BF16)
            n_done = lax.dot_general(ones, done.astype(BF16), _NT, preferred_element_type=F32)
            n_below = lax.dot_general(ones, below, _NT, preferred_element_type=F32)
            tok = jnp.minimum(n_done * LANES + n_below, n_tokens - 1.0)
            tok_v[slot][:, c * LANES:(c + 1) * LANES] = tok.astype(I32)
        tok_copy(slot).start()

    def gather_row(slot, r):
        tok = tok_s[slot][0, r]
        pltpu.make_async_copy(fpk_hbm.at[pl.ds(tok, 1), :], xs[slot].at[pl.ds(r, 1), :],
                              sem_x.at[slot]).start()

    def gather_start(slot, unrolled):
        if unrolled:
            for r in range(rows):
                gather_row(slot, r)
        else:
            def body(r, c):
                gather_row(slot, r)
                return c
            lax.fori_loop(0, rows, body, 0, unroll=8)

    def gather_wait(slot):
        pltpu.make_async_copy(fpk_hbm.at[pl.ds(0, rows), :], xs[slot], sem_x.at[slot]).wait()

    def weight_pairs(e):
        return ((wg_hbm.at[e], stage_g), (wu_hbm.at[e], stage_u), (wd_hbm.at[e], stage_d))

    def weights_start(e):
        for k, (src, dst) in enumerate(weight_pairs(e)):
            pltpu.async_copy(src, dst, sem_w.at[k], priority=1)

    def weights_wait(e):
        for k, (src, dst) in enumerate(weight_pairs(e)):
            pltpu.make_async_copy(src, dst, sem_w.at[k]).wait()

    def cast_weights(src, dst):
        chunk = math.gcd(256, src.shape[0])
        def body(c, carry):
            r0 = pl.multiple_of(c * chunk, chunk)
            dst[pl.ds(r0, chunk), :] = src[pl.ds(r0, chunk), :].astype(BF16)
            return carry
        lax.fori_loop(0, src.shape[0] // chunk, body, 0)

    @pl.when((i == 0) & (n_used > 0))
    def _():
        cum_copy = pltpu.make_async_copy(cum_hbm, cum_v, sem_cum.at[0])
        ends_copy = pltpu.make_async_copy(ends_hbm, ends_v, sem_cum.at[1])
        cum_copy.start()
        ends_copy.start()
        weights_start(be_ref[0])
        cum_copy.wait()
        ends_copy.wait()
        invert_start(0, 0)
        tok_copy(0).wait()
        gather_start(0, unrolled=False)
        invert_start(1, 1)

    def run_block(cur):
        nxt = 1 - cur
        gather_wait(cur)
        tok_copy(nxt).wait()

        per_group = rows // EXPERT_ISSUE_GROUPS
        groups = iter(range(EXPERT_ISSUE_GROUPS))

        def issue_group(fence):
            if fence:
                pl.semaphore_signal(sem_fence.at[0], 1)
                pl.semaphore_wait(sem_fence.at[0], 1)
            gi = next(groups)
            for r in range(gi * per_group, (gi + 1) * per_group):
                gather_row(nxt, r)

        issue_group(fence=False)
        invert_start(i + 2, cur)
        k_part = half // 2
        g = u = None
        for use_hi in (False, True):
            for part in range(2):
                lo, hi = _unpack_bf16_pair(xs[cur][:, part * k_part:(part + 1) * k_part])
                xk = (hi if use_hi else lo).astype(BF16)
                k0 = (half if use_hi else 0) + part * k_part
                gk = jnp.dot(xk, wg[k0:k0 + k_part, :], preferred_element_type=F32)
                uk = jnp.dot(xk, wu[k0:k0 + k_part, :], preferred_element_type=F32)
                g = gk if g is None else g + gk
                u = uk if u is None else u + uk
                issue_group(fence=True)
        act = (_silu(g) * u).astype(BF16)
        n_part = half // 4
        for q in range(4):
            c0 = q * n_part
            y_lo = jnp.dot(act, wd[:, c0:c0 + n_part], preferred_element_type=F32)
            y_hi = jnp.dot(act, wd[:, half + c0:half + c0 + n_part], preferred_element_type=F32)
            o_ref[:, c0:c0 + n_part] = _pack_bf16_pair(jnp.concatenate([y_lo, y_hi], axis=1))
            if q < 3:
                issue_group(fence=True)

        @pl.when(i == n_used - 1)
        def _():
            gather_wait(nxt)
            tok_copy(cur).wait()

    @pl.when(i < n_used)
    def _():
        @pl.when(first_ref[i] == 1)
        def _():
            weights_wait(be_ref[i])
            cast_weights(stage_g, wg)
            cast_weights(stage_u, wu)
            cast_weights(stage_d, wd)

            @pl.when(nexte_ref[i] >= 0)
            def _():
                weights_start(nexte_ref[i])

        @pl.when(i % 2 == 0)
        def _():
            run_block(0)

        @pl.when(i % 2 == 1)
        def _():
            run_block(1)

    @pl.when(i >= n_used)
    def _():
        o_ref[...] = jnp.zeros(o_ref.shape, o_ref.dtype)


def _routed_experts(fpk, cum, block_tables, w_gate, w_up, w_down):
    block_e, first, next_e, first_row, n_used = block_tables
    t, half = fpk.shape
    d = 2 * half
    n_exp, _, de = w_gate.shape
    rows = MOE_ROWS
    n_blocks = block_e.shape[0] - EXPERT_LOOKAHEAD
    n_buckets = t // LANES
    assert n_buckets <= LANES, "two-level token search covers at most 128 buckets of 128 tokens"
    cum3 = cum.reshape(n_exp, n_buckets, LANES)
    ends = jnp.pad(cum3[:, :, LANES - 1], ((0, 0), (0, LANES - n_buckets)),
                   constant_values=float(2 * t))
    any_spec = pl.BlockSpec(memory_space=pl.ANY)
    grid_spec = pltpu.PrefetchScalarGridSpec(
        num_scalar_prefetch=5,
        grid=(n_blocks,),
        in_specs=[any_spec, any_spec, any_spec, any_spec, any_spec, any_spec],
        out_specs=pl.BlockSpec((rows, half), lambda i, *_: (i, 0)),
        scratch_shapes=[pltpu.VMEM((n_exp, n_buckets, LANES), F32),
                        pltpu.VMEM((n_exp, LANES), F32),
                        pltpu.VMEM((SUBLANES, rows), I32),
                        pltpu.VMEM((SUBLANES, rows), I32),
                        pltpu.SMEM((SUBLANES, rows), I32),
                        pltpu.SMEM((SUBLANES, rows), I32),
                        pltpu.VMEM((rows, half), U32),
                        pltpu.VMEM((rows, half), U32),
                        pltpu.VMEM((d, de), F32),
                        pltpu.VMEM((d, de), F32),
                        pltpu.VMEM((de, d), F32),
                        pltpu.VMEM((d, de), BF16),
                        pltpu.VMEM((d, de), BF16),
                        pltpu.VMEM((de, d), BF16),
                        pltpu.SemaphoreType.DMA((2,)),
                        pltpu.SemaphoreType.DMA((2,)),
                        pltpu.SemaphoreType.DMA((2,)),
                        pltpu.SemaphoreType.DMA((3,)),
                        pltpu.SemaphoreType.REGULAR((1,))])
    return pl.pallas_call(
        functools.partial(_expert_kernel, rows=rows, n_tokens=t),
        grid_spec=grid_spec,
        out_shape=jax.ShapeDtypeStruct((n_blocks * rows, half), U32),
        compiler_params=_params("arbitrary"),
        name="routed_experts",
    )(block_e, first, next_e, first_row, n_used, cum3, ends, fpk, w_gate, w_up, w_down)


def _shared_kernel(fpk_ref, wg_ref, wu_ref, wd_ref, o_ref):
    half = fpk_ref.shape[1]
    lo, hi = _unpack_bf16_pair(fpk_ref[...])
    x_lo = lo.astype(BF16)
    x_hi = hi.astype(BF16)
    g = (jnp.dot(x_lo, wg_ref[0:half, :], preferred_element_type=F32)
         + jnp.dot(x_hi, wg_ref[half:2 * half, :], preferred_element_type=F32))
    u = (jnp.dot(x_lo, wu_ref[0:half, :], preferred_element_type=F32)
         + jnp.dot(x_hi, wu_ref[half:2 * half, :], preferred_element_type=F32))
    act = (_silu(g) * u).astype(BF16)
    o_ref[...] = jnp.dot(act, wd_ref[...], preferred_element_type=F32).astype(o_ref.dtype)


def _shared_expert(fpk, wg_bf, wu_bf, wd_bf):
    t, half = fpk.shape
    d = 2 * half
    ds = wg_bf.shape[1]
    tm = min(512, t)
    return pl.pallas_call(
        _shared_kernel,
        grid=(t // tm,),
        in_specs=[pl.BlockSpec((tm, half), lambda i: (i, 0)),
                  pl.BlockSpec((d, ds), lambda i: (0, 0)),
                  pl.BlockSpec((d, ds), lambda i: (0, 0)),
                  pl.BlockSpec((ds, d), lambda i: (0, 0))],
        out_specs=pl.BlockSpec((tm, d), lambda i: (i, 0)),
        out_shape=jax.ShapeDtypeStruct((t, d), BF16),
        compiler_params=_params("arbitrary"),
        name="shared_expert",
    )(fpk, wg_bf, wu_bf, wd_bf)


def _combine_kernel(dest_hbm, ys_hbm, wts_ref, sh_ref, h_ref, mod_ref, g_ref, o_ref,
                    dest_s0, dest_s1, gbuf0, gbuf1, sem_d, sem_g, *, tm, top_k, n_tiles):
    i = pl.program_id(0)
    n_rows = top_k * tm
    half = gbuf0.shape[1]
    dest_s, gbuf = (dest_s0, dest_s1), (gbuf0, gbuf1)
    last = n_tiles - 1

    def dest_copy(tile, s):
        return pltpu.make_async_copy(dest_hbm.at[tile], dest_s[s], sem_d.at[s])

    def gather_row(s, r, priority):
        row = dest_s[s][r]
        pltpu.async_copy(ys_hbm.at[pl.ds(row, 1), :], gbuf[s].at[pl.ds(r, 1), :], sem_g.at[s],
                         priority=priority)

    def gather_start(s, unrolled):
        if unrolled:
            for r in range(n_rows):
                gather_row(s, r, r % 2)
        else:
            def body(r, c):
                gather_row(s, r, 0)
                return c
            lax.fori_loop(0, n_rows, body, 0, unroll=8)

    def gather_wait(s):
        pltpu.make_async_copy(ys_hbm.at[pl.ds(0, n_rows), :], gbuf[s], sem_g.at[s]).wait()

    @pl.when(i == 0)
    def _():
        first_dest = dest_copy(0, 0)
        first_dest.start()
        first_dest.wait()
        gather_start(0, unrolled=False)
        dest_copy(jnp.minimum(1, last), 1).start()

    def run_tile(cur):
        nxt = 1 - cur
        gather_wait(cur)
        dest_copy(jnp.minimum(i + 1, last), nxt).wait()
        gather_start(nxt, unrolled=True)
        dest_copy(jnp.minimum(i + 2, last), cur).start()
        acc_lo = jnp.zeros((tm, half), F32)
        acc_hi = jnp.zeros((tm, half), F32)
        for k in range(top_k):
            lo, hi = _unpack_bf16_pair(gbuf[cur][k * tm:(k + 1) * tm, :])
            w = wts_ref[:, k:k + 1]
            acc_lo = acc_lo + w * lo
            acc_hi = acc_hi + w * hi
        moe_lo = acc_lo + sh_ref[:, 0:half].astype(F32)
        moe_hi = acc_hi + sh_ref[:, half:2 * half].astype(F32)
        ssq = (jnp.sum(moe_lo * moe_lo, axis=1, keepdims=True)
               + jnp.sum(moe_hi * moe_hi, axis=1, keepdims=True))
        r = lax.rsqrt(ssq * (1.0 / (2 * half)) + NORM_EPS)
        gate2 = mod_ref[0, 5:6, :]
        g = g_ref[...]
        o_ref[:, 0:half] = h_ref[:, 0:half] + gate2[:, 0:half] * (moe_lo * r * g[:, 0:half])
        o_ref[:, half:2 * half] = (h_ref[:, half:2 * half]
                                   + gate2[:, half:2 * half] * (moe_hi * r * g[:, half:2 * half]))

        @pl.when(i == last)
        def _():
            gather_wait(nxt)
            dest_copy(last, cur).wait()

    @pl.when(i % 2 == 0)
    def _():
        run_tile(0)

    @pl.when(i % 2 == 1)
    def _():
        run_tile(1)


def _combine(ys, dest_tiles, wts, shared, h, mod, g_post, seq, top_k):
    t, d = h.shape
    tm = COMBINE_TOKENS
    n_tiles = t // tm
    tiles_per_seq = seq // tm
    row = pl.BlockSpec((tm, d), lambda i: (i, 0))
    any_spec = pl.BlockSpec(memory_space=pl.ANY)
    return pl.pallas_call(
        functools.partial(_combine_kernel, tm=tm, top_k=top_k, n_tiles=n_tiles),
        grid=(n_tiles,),
        in_specs=[any_spec, any_spec,
                  pl.BlockSpec((tm, top_k), lambda i: (i, 0)),
                  row, row,
                  pl.BlockSpec((1, N_MOD, d), lambda i: (i // tiles_per_seq, 0, 0)),
                  pl.BlockSpec((1, d), lambda i: (0, 0))],
        out_specs=row,
        out_shape=jax.ShapeDtypeStruct((t, d), F32),
        scratch_shapes=[pltpu.SMEM((top_k * tm,), I32),
                        pltpu.SMEM((top_k * tm,), I32),
                        pltpu.VMEM((top_k * tm, d // 2), U32),
                        pltpu.VMEM((top_k * tm, d // 2), U32),
                        pltpu.SemaphoreType.DMA((2,)),
                        pltpu.SemaphoreType.DMA((2,))],
        compiler_params=_params("arbitrary"),
        name="combine",
    )(dest_tiles, ys, wts, shared, h, mod, g_post.reshape(1, d))


def _rope_tables(seq):
    half = HEAD_DIM // 2
    inv_freq = jnp.exp(-math.log(ROPE_THETA) * jnp.arange(half, dtype=F32) / half)
    ang = jnp.arange(seq, dtype=F32)[:, None] * inv_freq[None, :]
    cos, sin = jnp.cos(ang), jnp.sin(ang)
    return jnp.concatenate([cos, cos], axis=1), jnp.concatenate([-sin, sin], axis=1)


def kernel(x, c, w_mod, b_mod, g_pre_mix, g_post_mix, g_pre_ffn, g_post_ffn, w_in, w_pool, pool_scale, g_attn_out, g_pool_out, w_out, w_router, router_bias, w_gate, w_up, w_down, w_shared_gate, w_shared_up, w_shared_down):
    batch, seq, d = x.shape
    depth = w_mod.shape[0]
    n_groups, gd = w_pool.shape[1], w_pool.shape[2]
    pool_width = n_groups * gd
    attn_width = (w_in.shape[2] - pool_width) // 3
    n_heads = attn_width // HEAD_DIM
    assert n_groups == len(POOL_WINDOWS) and seq % (ATTN_BLOCK * DILATIONS[-1]) == 0
    cosf, sinf = _rope_tables(seq)

    h = x.reshape(batch * seq, d)
    for layer in range(depth):
        mod = _modulation(c, w_mod[layer], b_mod[layer])
        proj = _in_projection(h, g_pre_mix[layer], mod, w_in[layer].astype(BF16), cosf, sinf,
                              seq, attn_width)
        attn = _dilated_attention(proj, batch, seq, n_heads)
        mixed = _mixer_norms(attn, proj, w_pool[layer].astype(BF16), pool_scale[layer],
                             g_attn_out[layer], g_pool_out[layer], batch, seq)
        y = _matmul(mixed, w_out[layer].astype(BF16), F32)
        h, fpk, idx_t, wts_t, rank_t, cum = _post_mix_route(
            y, h, mod, g_post_mix[layer], g_pre_ffn[layer], w_router[layer], router_bias[layer], seq)
        block_tables, dest_tiles = _dispatch_tables(idx_t, rank_t, cum, MOE_ROWS, COMBINE_TOKENS)
        ys = _routed_experts(fpk, cum, block_tables, w_gate[layer], w_up[layer], w_down[layer])
        shared = _shared_expert(fpk, w_shared_gate[layer].astype(BF16),
                                w_shared_up[layer].astype(BF16), w_shared_down[layer].astype(BF16))
        h = _combine(ys, dest_tiles, wts_t.T, shared, h, mod, g_post_ffn[layer], seq, TOP_K)
    return h.reshape(batch, seq, d)
```

```python
import functools
import math

import jax
import jax.numpy as jnp
from jax import lax
from jax.experimental import pallas as pl
from jax.experimental.pallas import tpu as pltpu

F32 = jnp.float32
BF16 = jnp.bfloat16
I32 = jnp.int32
U32 = jnp.uint32

HEAD_DIM = 128
ATTN_BLOCK = 128
DILATIONS = (1, 4, 16)
POOL_WINDOWS = (2, 4, 8, 16)
POOL_HALO = 16
ROPE_THETA = 10000.0
TOP_K = 8
N_EXPERT_GROUPS = 8
TOPK_GROUPS = 4
ROUTED_SCALE = 2.5
N_MOD = 6
NORM_EPS = 1e-6
MASK_VALUE = -1e30

LANES = 128
SUBLANES = 8
VMEM_LIMIT_BYTES = 60 * 1024 * 1024

MOE_ROWS = 256
EXPERT_LOOKAHEAD = 2
COMBINE_TOKENS = 128

_NT = (((1,), (1,)), ((), ()))


def _params(*sem):
    return pltpu.CompilerParams(dimension_semantics=sem, vmem_limit_bytes=VMEM_LIMIT_BYTES)


def _rms(x, g):
    return x * lax.rsqrt(jnp.mean(x * x, axis=-1, keepdims=True) + NORM_EPS) * g


def _silu(x):
    return x * jax.nn.sigmoid(x)


def _pack_bf16_pair(x):
    n = x.shape[1] // 2
    lo = lax.bitcast_convert_type(x[:, :n].astype(BF16).astype(F32), U32)
    hi = lax.bitcast_convert_type(x[:, n:].astype(BF16).astype(F32), U32)
    return (lo >> 16) | hi


def _unpack_bf16_pair(w):
    lo = lax.bitcast_convert_type(w << 16, F32)
    hi = lax.bitcast_convert_type(w & jnp.uint32(0xFFFF0000), F32)
    return lo, hi


def _mod_kernel(c_ref, w_ref, b_ref, o_ref):
    cond = _silu(c_ref[...])
    o_ref[...] = jnp.dot(cond.astype(BF16), w_ref[...].astype(BF16),
                         preferred_element_type=F32) + b_ref[...]


def _modulation(c, w_mod, b_mod):
    b, d = c.shape
    n = w_mod.shape[1]
    bp = -(-b // SUBLANES) * SUBLANES
    tn = 512
    out = pl.pallas_call(
        _mod_kernel,
        grid=(n // tn,),
        in_specs=[pl.BlockSpec((bp, d), lambda j: (0, 0)),
                  pl.BlockSpec((d, tn), lambda j: (0, j)),
                  pl.BlockSpec((1, tn), lambda j: (0, j))],
        out_specs=pl.BlockSpec((bp, tn), lambda j: (0, j)),
        out_shape=jax.ShapeDtypeStruct((bp, n), F32),
        compiler_params=_params("arbitrary"),
        name="modulation",
    )(jnp.pad(c, ((0, bp - b), (0, 0))), w_mod, b_mod.reshape(1, n))
    return out[:b].reshape(b, N_MOD, d)


def _inproj_kernel(x_ref, g_ref, mod_ref, w_ref, cos_ref, sin_ref, o_ref, a_scr, *,
                   n_q_tiles, n_rope_tiles):
    j = pl.program_id(1)

    @pl.when(j == 0)
    def _():
        shift1 = mod_ref[0, 0:1, :]
        scale1 = mod_ref[0, 1:2, :]
        a = _rms(x_ref[...], g_ref[...]) * (1.0 + scale1) + shift1
        a_scr[...] = a.astype(BF16)

    acc = jnp.dot(a_scr[...], w_ref[...], preferred_element_type=F32)

    @pl.when(j < n_rope_tiles)
    def _():
        qk_scale = jnp.where(j < n_q_tiles, HEAD_DIM ** -0.5, 1.0)
        cosf = cos_ref[...] * qk_scale
        sinf = sin_ref[...] * qk_scale
        for hh in range(acc.shape[1] // HEAD_DIM):
            blk = acc[:, hh * HEAD_DIM:(hh + 1) * HEAD_DIM]
            rot = blk * cosf + pltpu.roll(blk, HEAD_DIM // 2, 1) * sinf
            o_ref[:, hh * HEAD_DIM:(hh + 1) * HEAD_DIM] = rot.astype(o_ref.dtype)

    @pl.when(j >= n_rope_tiles)
    def _():
        o_ref[...] = acc.astype(o_ref.dtype)


def _in_projection(x2, g, mod, w_in_bf, cosf, sinf, seq, attn_width):
    t, d = x2.shape
    n = w_in_bf.shape[1]
    tm = min(512, seq)
    tn = min(1024, attn_width)
    tiles_per_seq = seq // tm
    return pl.pallas_call(
        functools.partial(_inproj_kernel, n_q_tiles=attn_width // tn, n_rope_tiles=2 * attn_width // tn),
        grid=(t // tm, n // tn),
        in_specs=[pl.BlockSpec((tm, d), lambda i, j: (i, 0)),
                  pl.BlockSpec((1, d), lambda i, j: (0, 0)),
                  pl.BlockSpec((1, N_MOD, d), lambda i, j: (i // tiles_per_seq, 0, 0)),
                  pl.BlockSpec((d, tn), lambda i, j: (0, j)),
                  pl.BlockSpec((tm, HEAD_DIM), lambda i, j: (i % tiles_per_seq, 0)),
                  pl.BlockSpec((tm, HEAD_DIM), lambda i, j: (i % tiles_per_seq, 0))],
        out_specs=pl.BlockSpec((tm, tn), lambda i, j: (i, j)),
        out_shape=jax.ShapeDtypeStruct((t, n), BF16),
        scratch_shapes=[pltpu.VMEM((tm, d), BF16)],
        compiler_params=_params("arbitrary", "arbitrary"),
        name="in_projection",
    )(x2, g.reshape(1, d), mod, w_in_bf, cosf, sinf)


def _attn_kernel(q_ref, k_ref, v_ref, o_ref, qf, kf, vf, qs, ks, vs, accs, maxs, dens, stage, *, seq):
    blk = ATTN_BLOCK
    n_blk = seq // blk
    qf[...] = q_ref[...].astype(F32)
    kf[...] = k_ref[...].astype(F32)
    vf[...] = v_ref[...].astype(F32)
    vs[:, HEAD_DIM:2 * HEAD_DIM] = jnp.ones((vs.shape[0], HEAD_DIM), BF16)

    qi = lax.broadcasted_iota(I32, (blk, blk), 0)
    ki = lax.broadcasted_iota(I32, (blk, blk), 1)
    causal_bias = jnp.where(ki <= qi, 0.0, MASK_VALUE)
    upper_bias = jnp.where(ki >= qi, 0.0, MASK_VALUE)

    for bi, dil in enumerate(DILATIONS):
        sub_len = seq // dil
        nbk = sub_len // blk
        lead = blk if nbk > 1 else 0
        pitch = sub_len + lead
        def to_sub_order(res, carry, dil=dil, sub_len=sub_len, lead=lead, pitch=pitch):
            base = pl.multiple_of(jnp.asarray(res * pitch, I32), blk)
            q0 = pl.multiple_of(jnp.asarray(res * sub_len, I32), blk)
            if dil == 1:
                q_src, k_src, v_src = q_ref[...], k_ref[...], v_ref[...]
            else:
                q_src = qf[pl.ds(res, sub_len, stride=dil), :].astype(BF16)
                k_src = kf[pl.ds(res, sub_len, stride=dil), :].astype(BF16)
                v_src = vf[pl.ds(res, sub_len, stride=dil), :].astype(BF16)
            qs[pl.ds(q0, sub_len), :] = q_src
            if lead:
                ks[pl.ds(base, lead), :] = jnp.zeros((lead, HEAD_DIM), BF16)
                vs[pl.ds(base, lead), 0:HEAD_DIM] = jnp.zeros((lead, HEAD_DIM), BF16)
            ks[pl.ds(base + lead, sub_len), :] = k_src
            vs[pl.ds(base + lead, sub_len), 0:HEAD_DIM] = v_src
            return carry

        if dil == 1:
            to_sub_order(0, 0)
        else:
            lax.fori_loop(0, dil, to_sub_order, 0)

        def blocks(ref, offset, dil=dil, pitch=pitch, sub_len=sub_len, nbk=nbk):
            parts = [ref[res * pitch + offset:res * pitch + offset + sub_len, :]
                     .reshape(nbk, blk, ref.shape[1]) for res in range(dil)]
            return parts[0] if dil == 1 else jnp.concatenate(parts, axis=0)

        q3 = qs[...].reshape(n_blk, blk, HEAD_DIM)
        s_own = jnp.einsum('nqc,nkc->nqk', q3, blocks(ks, lead),
                           preferred_element_type=F32) + causal_bias[None]
        if nbk > 1:
            first = (lax.broadcasted_iota(I32, (n_blk, blk, blk), 0) & (nbk - 1)) == 0
            prev_bias = jnp.where(first, MASK_VALUE, upper_bias[None])
            s_prev = jnp.einsum('nqc,nkc->nqk', q3, blocks(ks, 0),
                                preferred_element_type=F32) + prev_bias
            m = jnp.max(jnp.maximum(s_own, s_prev), axis=-1, keepdims=True)
            accx = (jnp.einsum('nqk,nkc->nqc', jnp.exp(s_own - m).astype(BF16), blocks(vs, lead),
                               preferred_element_type=F32)
                    + jnp.einsum('nqk,nkc->nqc', jnp.exp(s_prev - m).astype(BF16), blocks(vs, 0),
                                 preferred_element_type=F32))
        else:
            m = jnp.max(s_own, axis=-1, keepdims=True)
            accx = jnp.einsum('nqk,nkc->nqc', jnp.exp(s_own - m).astype(BF16), blocks(vs, lead),
                              preferred_element_type=F32)
        m_b = jnp.broadcast_to(m, (n_blk, blk, HEAD_DIM))
        results = (accx[:, :, 0:HEAD_DIM], accx[:, :, HEAD_DIM:2 * HEAD_DIM], m_b)
        if dil == 1:
            for dst, val in zip((accs, dens, maxs), results):
                dst[bi] = val.reshape(seq, HEAD_DIM)
        else:
            for j, val in enumerate(results):
                stage[j] = val.reshape(seq, HEAD_DIM)

            def to_natural_order(res, carry, dil=dil, sub_len=sub_len, bi=bi):
                r0 = pl.multiple_of(jnp.asarray(res * sub_len, I32), blk)
                for j, dst in enumerate((accs, dens, maxs)):
                    dst[bi, pl.ds(res, sub_len, stride=dil), :] = stage[j, pl.ds(r0, sub_len), :]
                return carry

            lax.fori_loop(0, dil, to_natural_order, 0)

    def merge(c, carry):
        r0 = pl.multiple_of(c * blk, blk)
        ms = [maxs[bi, pl.ds(r0, blk), :] for bi in range(len(DILATIONS))]
        m_all = functools.reduce(jnp.maximum, ms)
        ws = [jnp.exp(m - m_all) for m in ms]
        den = sum(w * dens[bi, pl.ds(r0, blk), :] for bi, w in enumerate(ws))
        num = sum(w * accs[bi, pl.ds(r0, blk), :] for bi, w in enumerate(ws))
        o_ref[pl.ds(r0, blk), :] = (num / den).astype(o_ref.dtype)
        return carry

    lax.fori_loop(0, n_blk, merge, 0, unroll=2)


def _dilated_attention(proj, batch, seq, n_heads):
    t = proj.shape[0]
    nb = len(DILATIONS)
    key_rows = max(d * (seq // d + (ATTN_BLOCK if seq // d > ATTN_BLOCK else 0)) for d in DILATIONS)
    return pl.pallas_call(
        functools.partial(_attn_kernel, seq=seq),
        grid=(batch, n_heads),
        in_specs=[pl.BlockSpec((seq, HEAD_DIM), lambda b, h: (b, h)),
                  pl.BlockSpec((seq, HEAD_DIM), lambda b, h: (b, n_heads + h)),
                  pl.BlockSpec((seq, HEAD_DIM), lambda b, h: (b, 2 * n_heads + h))],
        out_specs=pl.BlockSpec((seq, HEAD_DIM), lambda b, h: (b, h)),
        out_shape=jax.ShapeDtypeStruct((t, n_heads * HEAD_DIM), BF16),
        scratch_shapes=[pltpu.VMEM((seq, HEAD_DIM), F32),
                        pltpu.VMEM((seq, HEAD_DIM), F32),
                        pltpu.VMEM((seq, HEAD_DIM), F32),
                        pltpu.VMEM((seq, HEAD_DIM), BF16),
                        pltpu.VMEM((key_rows, HEAD_DIM), BF16),
                        pltpu.VMEM((key_rows, 2 * HEAD_DIM), BF16),
                        pltpu.VMEM((nb, seq, HEAD_DIM), F32),
                        pltpu.VMEM((nb, seq, HEAD_DIM), F32),
                        pltpu.VMEM((nb, seq, HEAD_DIM), F32),
                        pltpu.VMEM((3, seq, HEAD_DIM), F32)],
        compiler_params=_params("arbitrary", "arbitrary"),
        name="dilated_attention",
    )(proj, proj, proj)


def _mix_kernel(attn_ref, u_ref, wp_ref, ps_ref, ga_ref, gp_ref, o_ref, ext, *, tr, aw, gd):
    j = pl.program_id(1)
    n_groups = len(POOL_WINDOWS)

    @pl.when(j == 0)
    def _():
        ext[0:POOL_HALO, :] = jnp.zeros((POOL_HALO, ext.shape[1]), BF16)

    @pl.when(j > 0)
    def _():
        ext[0:POOL_HALO, :] = ext[tr:tr + POOL_HALO, :]

    ext[POOL_HALO:POOL_HALO + tr, :] = u_ref[...]

    o_ref[:, 0:aw] = _rms(attn_ref[...].astype(F32), ga_ref[...]).astype(o_ref.dtype)

    pos = j * tr + lax.broadcasted_iota(I32, (tr, 1), 0)
    diff = (lax.broadcasted_iota(I32, (tr, tr + POOL_HALO), 0) + POOL_HALO
            - lax.broadcasted_iota(I32, (tr, tr + POOL_HALO), 1))
    ys = []
    ssq = jnp.zeros((tr, 1), F32)
    for g, w in enumerate(POOL_WINDOWS):
        band = ((diff >= 0) & (diff < w)).astype(BF16)
        win = jnp.dot(band, ext[:, g * gd:(g + 1) * gd], preferred_element_type=F32)
        cnt = jnp.minimum(pos + 1, w).astype(F32)
        mix = win * (1.0 / cnt) - u_ref[:, g * gd:(g + 1) * gd].astype(F32)
        y = jnp.dot(mix.astype(BF16), wp_ref[g], preferred_element_type=F32) * ps_ref[g:g + 1, :]
        ys.append(y)
        ssq = ssq + jnp.sum(y * y, axis=1, keepdims=True)
    r = lax.rsqrt(ssq * (1.0 / (n_groups * gd)) + NORM_EPS)
    for g in range(n_groups):
        o_ref[:, aw + g * gd:aw + (g + 1) * gd] = (
            ys[g] * r * gp_ref[:, g * gd:(g + 1) * gd]).astype(o_ref.dtype)


def _mixer_norms(attn, proj, w_pool_bf, pool_scale, g_attn, g_pool, batch, seq):
    t, aw = attn.shape
    n_groups, gd, _ = w_pool_bf.shape
    pw = n_groups * gd
    tr = min(256, seq)
    tiles = seq // tr
    u_col_block = (proj.shape[1] - pw) // pw
    return pl.pallas_call(
        functools.partial(_mix_kernel, tr=tr, aw=aw, gd=gd),
        grid=(batch, tiles),
        in_specs=[pl.BlockSpec((tr, aw), lambda b, j: (b * tiles + j, 0)),
                  pl.BlockSpec((tr, pw), lambda b, j: (b * tiles + j, u_col_block)),
                  pl.BlockSpec((n_groups, gd, gd), lambda b, j: (0, 0, 0)),
                  pl.BlockSpec((n_groups, gd), lambda b, j: (0, 0)),
                  pl.BlockSpec((1, aw), lambda b, j: (0, 0)),
                  pl.BlockSpec((1, pw), lambda b, j: (0, 0))],
        out_specs=pl.BlockSpec((tr, aw + pw), lambda b, j: (b * tiles + j, 0)),
        out_shape=jax.ShapeDtypeStruct((t, aw + pw), BF16),
        scratch_shapes=[pltpu.VMEM((tr + POOL_HALO, pw), BF16)],
        compiler_params=_params("arbitrary", "arbitrary"),
        name="mixer_norms",
    )(attn, proj, w_pool_bf, pool_scale, g_attn.reshape(1, aw), g_pool.reshape(1, pw))


def _matmul_kernel(a_ref, b_ref, o_ref):
    o_ref[...] = jnp.dot(a_ref[...], b_ref[...], preferred_element_type=F32).astype(o_ref.dtype)


def _matmul(a, b, out_dtype, tm=1024, tn=1024):
    m, k = a.shape
    n = b.shape[1]
    tm, tn = min(tm, m), min(tn, n)
    return pl.pallas_call(
        _matmul_kernel,
        grid=(m // tm, n // tn),
        in_specs=[pl.BlockSpec((tm, k), lambda i, j: (i, 0)),
                  pl.BlockSpec((k, tn), lambda i, j: (0, j))],
        out_specs=pl.BlockSpec((tm, tn), lambda i, j: (i, j)),
        out_shape=jax.ShapeDtypeStruct((m, n), out_dtype),
        compiler_params=_params("arbitrary", "arbitrary"),
        name="out_projection",
    )(a, b)


def _split_bf16(x):
    hi = x.astype(BF16)
    lo = (x - hi.astype(F32)).astype(BF16)
    return hi, lo


def _route(f, w_t, bias):
    f_hi, f_lo = _split_bf16(f)
    w_hi, w_lo = _split_bf16(w_t)
    logits = (lax.dot_general(w_hi, f_hi, _NT, preferred_element_type=F32)
              + lax.dot_general(w_hi, f_lo, _NT, preferred_element_type=F32)
              + lax.dot_general(w_lo, f_hi, _NT, preferred_element_type=F32))
    n_exp, tm = logits.shape
    per = n_exp // N_EXPERT_GROUPS
    scores = jax.nn.sigmoid(logits)
    biased = scores + bias
    ninf = -jnp.inf

    b3 = biased.reshape(N_EXPERT_GROUPS, per, tm)
    memb = lax.broadcasted_iota(I32, b3.shape, 1)
    m1 = jnp.max(b3, axis=1, keepdims=True)
    i1 = jnp.min(jnp.where(b3 == m1, memb, per), axis=1, keepdims=True)
    m2 = jnp.max(jnp.where(memb == i1, ninf, b3), axis=1, keepdims=True)
    gscore = (m1 + m2).reshape(N_EXPERT_GROUPS, tm)

    gid = lax.broadcasted_iota(I32, gscore.shape, 0)
    gmask = jnp.zeros(gscore.shape, jnp.bool_)
    cur = gscore
    for _ in range(TOPK_GROUPS):
        mx = jnp.max(cur, axis=0, keepdims=True)
        ix = jnp.min(jnp.where(cur == mx, gid, N_EXPERT_GROUPS), axis=0, keepdims=True)
        sel = gid == ix
        gmask = gmask | sel
        cur = jnp.where(sel, ninf, cur)
    emask = jnp.broadcast_to(gmask.reshape(N_EXPERT_GROUPS, 1, tm), b3.shape).reshape(n_exp, tm)

    masked = jnp.where(emask, biased, ninf)
    eid = lax.broadcasted_iota(I32, masked.shape, 0)
    idx_rows, w_rows, sels = [], [], []
    chosen = jnp.zeros(masked.shape, F32)
    for _ in range(TOP_K):
        mx = jnp.max(masked, axis=0, keepdims=True)
        ix = jnp.min(jnp.where(masked == mx, eid, n_exp), axis=0, keepdims=True)
        sel = eid == ix
        idx_rows.append(ix)
        w_rows.append(jnp.sum(jnp.where(sel, scores, 0.0), axis=0, keepdims=True))
        sels.append(sel)
        chosen = jnp.where(sel, 1.0, chosen)
        masked = jnp.where(sel, ninf, masked)
    wsel = jnp.concatenate(w_rows, axis=0)
    wts = wsel / jnp.sum(wsel, axis=0, keepdims=True) * ROUTED_SCALE
    return jnp.concatenate(idx_rows, axis=0), wts, sels, chosen


def _postmix_kernel(y_ref, x_ref, mod_ref, gpost_ref, gpre_ref, wt_ref, bias_ref,
                    h_ref, fpk_ref, idx_ref, wts_ref, rank_ref, cum_ref, carry):
    i = pl.program_id(0)

    @pl.when(i == 0)
    def _():
        carry[...] = jnp.zeros(carry.shape, F32)

    gate1 = mod_ref[0, 2:3, :]
    shift2 = mod_ref[0, 3:4, :]
    scale2 = mod_ref[0, 4:5, :]
    h = x_ref[...] + gate1 * _rms(y_ref[...], gpost_ref[...])
    h_ref[...] = h
    f = _rms(h, gpre_ref[...]) * (1.0 + scale2) + shift2
    fpk_ref[...] = _pack_bf16_pair(f)

    idx, wts, sels, chosen = _route(f, wt_ref[...], bias_ref[...])
    tm = f.shape[0]
    upper = (lax.broadcasted_iota(I32, (tm, tm), 0)
             <= lax.broadcasted_iota(I32, (tm, tm), 1)).astype(BF16)
    cum = jnp.dot(chosen.astype(BF16), upper, preferred_element_type=F32) + carry[...]
    carry[...] = cum[:, tm - 1:tm]
    idx_ref[...] = idx
    wts_ref[...] = wts
    cum_ref[...] = cum
    rank_ref[...] = jnp.concatenate(
        [jnp.sum(jnp.where(sel, cum, 0.0), axis=0, keepdims=True) for sel in sels],
        axis=0).astype(I32) - 1


def _post_mix_route(y, x2, mod, g_post, g_pre, w_router, router_bias, seq):
    t, d = x2.shape
    n_exp = w_router.shape[1]
    tm = min(256, seq)
    tiles_per_seq = seq // tm
    row = pl.BlockSpec((tm, d), lambda i: (i, 0))
    vec = pl.BlockSpec((1, d), lambda i: (0, 0))
    kt = pl.BlockSpec((TOP_K, tm), lambda i: (0, i))
    return pl.pallas_call(
        _postmix_kernel,
        grid=(t // tm,),
        in_specs=[row, row,
                  pl.BlockSpec((1, N_MOD, d), lambda i: (i // tiles_per_seq, 0, 0)),
                  vec, vec,
                  pl.BlockSpec((n_exp, d), lambda i: (0, 0)),
                  pl.BlockSpec((n_exp, 1), lambda i: (0, 0))],
        out_specs=[row, pl.BlockSpec((tm, d // 2), lambda i: (i, 0)), kt, kt, kt,
                   pl.BlockSpec((n_exp, tm), lambda i: (0, i))],
        out_shape=[jax.ShapeDtypeStruct((t, d), F32),
                   jax.ShapeDtypeStruct((t, d // 2), U32),
                   jax.ShapeDtypeStruct((TOP_K, t), I32),
                   jax.ShapeDtypeStruct((TOP_K, t), F32),
                   jax.ShapeDtypeStruct((TOP_K, t), I32),
                   jax.ShapeDtypeStruct((n_exp, t), F32)],
        scratch_shapes=[pltpu.VMEM((n_exp, 1), F32)],
        compiler_params=_params("arbitrary"),
        name="post_mix_route",
    )(y, x2, mod, g_post.reshape(1, d), g_pre.reshape(1, d), w_router.T, router_bias.reshape(n_exp, 1))


def _dispatch_tables(idx_t, rank_t, cum, rows_per_block, tokens_per_tile):
    k, t = idx_t.shape
    n_exp = cum.shape[0]
    n = k * t
    n_blocks = n // rows_per_block + n_exp
    experts = jnp.arange(n_exp, dtype=I32)
    counts = cum[:, -1].astype(I32)
    padded = (counts + rows_per_block - 1) // rows_per_block * rows_per_block
    pad_ends = jnp.cumsum(padded)
    pad_starts = pad_ends - padded
    n_used = (pad_ends[-1] // rows_per_block).astype(I32)
    n_tab = n_blocks + EXPERT_LOOKAHEAD
    blk_start = jnp.arange(n_tab, dtype=I32) * rows_per_block
    block_e = jnp.minimum(jnp.sum((pad_ends[None, :] <= blk_start[:, None]).astype(I32), axis=1),
                          n_exp - 1).astype(I32)
    of_block = block_e[:, None] == experts[None, :]
    first_row = blk_start - jnp.sum(jnp.where(of_block, pad_starts[None, :], 0), axis=1)
    prev_e = jnp.concatenate([jnp.full((1,), -1, I32), block_e[:-1]])
    first = (block_e != prev_e).astype(I32)
    after = jnp.sum(jnp.where(of_block, pad_ends[None, :], 0), axis=1) // rows_per_block
    next_e = jnp.where(after < n_used, block_e[jnp.minimum(after, n_tab - 1)], -1).astype(I32)
    dest = jnp.sum(jnp.where(idx_t[:, :, None] == experts[None, None, :],
                             pad_starts[None, None, :], 0), axis=2) + rank_t
    n_tiles = t // tokens_per_tile
    dest_tiles = dest.reshape(k, n_tiles, tokens_per_tile).transpose(1, 0, 2).reshape(
        n_tiles, k * tokens_per_tile).astype(I32)
    return (block_e, first, next_e, first_row.astype(I32), n_used.reshape(1)), dest_tiles


def _expert_kernel(be_ref, first_ref, nexte_ref, row0_ref, nused_ref,
                   cum_hbm, ends_hbm, fpk_hbm, wg_hbm, wu_hbm, wd_hbm,
                   o_ref,
                   cum_v, ends_v, tok_v0, tok_v1, tok_s0, tok_s1, xs0, xs1,
                   stage_g, stage_u, stage_d, wg, wu, wd,
                   sem_cum, sem_tok, sem_x, sem_w, sem_fence, *, rows, n_tokens):
    i = pl.program_id(0)
    n_used = nused_ref[0]
    half = wd.shape[1] // 2
    n_buckets = cum_v.shape[1]
    tok_v, tok_s, xs = (tok_v0, tok_v1), (tok_s0, tok_s1), (xs0, xs1)

    def tok_copy(slot):
        return pltpu.make_async_copy(tok_v[slot], tok_s[slot], sem_tok.at[slot])

    def invert(blk, slot):
        e = be_ref[blk]
        row0 = row0_ref[blk]
        ends = ends_v[pl.ds(e, 1), :]
        counts = cum_v[e]
        c_hi = jnp.floor(counts * (1.0 / LANES))
        c_lo = counts - c_hi * LANES
        if n_buckets < LANES:
            fill = jnp.zeros((LANES - n_buckets, LANES), F32)
            c_hi = jnp.concatenate([c_hi, fill], axis=0)
            c_lo = jnp.concatenate([c_lo, fill], axis=0)
        c_hi = c_hi.astype(BF16)
        c_lo = c_lo.astype(BF16)
        ones = jnp.ones((SUBLANES, LANES), BF16)
        lane = lax.broadcasted_iota(I32, (LANES, LANES), 1)
        for c in range(rows // LANES):
            j = (row0 + c * LANES + lax.broadcasted_iota(I32, (LANES, 1), 0)).astype(F32)
            done = jnp.where(ends <= j, 1.0, 0.0)
            before = jnp.where(lane == 0, 1.0, pltpu.roll(done, 1, 1))
            pick = ((1.0 - done) * before).astype(BF16)
            in_bucket = (jnp.dot(pick, c_hi, preferred_element_type=F32) * LANES
                         + jnp.dot(pick, c_lo, preferred_element_type=F32))
            below = jnp.where(in_bucket <= j, 1.0, 0.0).astype(BF16)
            n_done = lax.dot_general(ones, done.astype(BF16), _NT, preferred_element_type=F32)
            n_below = lax.dot_general(ones, below, _NT, preferred_element_type=F32)
            tok = jnp.minimum(n_done * LANES + n_below, n_tokens - 1.0)
            tok_v[slot][:, c * LANES:(c + 1) * LANES] = tok.astype(I32)

    def gather_row(slot, r):
        tok = tok_s[slot][0, r]
        pltpu.make_async_copy(fpk_hbm.at[pl.ds(tok, 1), :], xs[slot].at[pl.ds(r, 1), :],
                              sem_x.at[slot]).start()

    def gather_start(slot, unrolled):
        if unrolled:
            for r in range(rows):
                gather_row(slot, r)
        else:
            def body(r, c):
                gather_row(slot, r)
                return c
            lax.fori_loop(0, rows, body, 0, unroll=8)

    def gather_wait(slot):
        pltpu.make_async_copy(fpk_hbm.at[pl.ds(0, rows), :], xs[slot], sem_x.at[slot]).wait()

    def weight_pairs(e):
        return ((wg_hbm.at[e], stage_g), (wu_hbm.at[e], stage_u), (wd_hbm.at[e], stage_d))

    def weights_start(e):
        for k, (src, dst) in enumerate(weight_pairs(e)):
            pltpu.async_copy(src, dst, sem_w.at[k], priority=1)

    def weights_wait(e):
        for k, (src, dst) in enumerate(weight_pairs(e)):
            pltpu.make_async_copy(src, dst, sem_w.at[k]).wait()

    def cast_weights(src, dst):
        chunk = math.gcd(256, src.shape[0])
        def body(c, carry):
            r0 = pl.multiple_of(c * chunk, chunk)
            dst[pl.ds(r0, chunk), :] = src[pl.ds(r0, chunk), :].astype(BF16)
            return carry
        lax.fori_loop(0, src.shape[0] // chunk, body, 0)

    @pl.when((i == 0) & (n_used > 0))
    def _():
        cum_copy = pltpu.make_async_copy(cum_hbm, cum_v, sem_cum.at[0])
        ends_copy = pltpu.make_async_copy(ends_hbm, ends_v, sem_cum.at[1])
        cum_copy.start()
        ends_copy.start()
        weights_start(be_ref[0])
        cum_copy.wait()
        ends_copy.wait()
        invert(0, 0)
        tok_copy(0).start()
        tok_copy(0).wait()
        gather_start(0, unrolled=False)
        invert(1, 1)
        tok_copy(1).start()

    def run_block(cur):
        nxt = 1 - cur
        gather_wait(cur)
        tok_copy(nxt).wait()
        gather_start(nxt, unrolled=True)
        invert(i + 2, cur)
        lo, hi = _unpack_bf16_pair(xs[cur][...])
        x_lo = lo.astype(BF16)
        x_hi = hi.astype(BF16)
        g = (jnp.dot(x_lo, wg[0:half, :], preferred_element_type=F32)
             + jnp.dot(x_hi, wg[half:2 * half, :], preferred_element_type=F32))
        u = (jnp.dot(x_lo, wu[0:half, :], preferred_element_type=F32)
             + jnp.dot(x_hi, wu[half:2 * half, :], preferred_element_type=F32))
        act = (_silu(g) * u).astype(BF16)
        pl.semaphore_signal(sem_fence.at[0], 1)
        pl.semaphore_wait(sem_fence.at[0], 1)
        tok_copy(cur).start()
        o_ref[...] = _pack_bf16_pair(jnp.dot(act, wd[...], preferred_element_type=F32))

        @pl.when(i == n_used - 1)
        def _():
            gather_wait(nxt)
            tok_copy(cur).wait()

    @pl.when(i < n_used)
    def _():
        @pl.when(first_ref[i] == 1)
        def _():
            weights_wait(be_ref[i])
            cast_weights(stage_g, wg)
            cast_weights(stage_u, wu)
            cast_weights(stage_d, wd)

            @pl.when(nexte_ref[i] >= 0)
            def _():
                weights_start(nexte_ref[i])

        @pl.when(i % 2 == 0)
        def _():
            run_block(0)

        @pl.when(i % 2 == 1)
        def _():
            run_block(1)

    @pl.when(i >= n_used)
    def _():
        o_ref[...] = jnp.zeros(o_ref.shape, o_ref.dtype)


def _routed_experts(fpk, cum, block_tables, w_gate, w_up, w_down):
    block_e, first, next_e, first_row, n_used = block_tables
    t, half = fpk.shape
    d = 2 * half
    n_exp, _, de = w_gate.shape
    rows = MOE_ROWS
    n_blocks = block_e.shape[0] - EXPERT_LOOKAHEAD
    n_buckets = t // LANES
    assert n_buckets <= LANES, "two-level token search covers at most 128 buckets of 128 tokens"
    cum3 = cum.reshape(n_exp, n_buckets, LANES)
    ends = jnp.pad(cum3[:, :, LANES - 1], ((0, 0), (0, LANES - n_buckets)),
                   constant_values=float(2 * t))
    any_spec = pl.BlockSpec(memory_space=pl.ANY)
    grid_spec = pltpu.PrefetchScalarGridSpec(
        num_scalar_prefetch=5,
        grid=(n_blocks,),
        in_specs=[any_spec, any_spec, any_spec, any_spec, any_spec, any_spec],
        out_specs=pl.BlockSpec((rows, half), lambda i, *_: (i, 0)),
        scratch_shapes=[pltpu.VMEM((n_exp, n_buckets, LANES), F32),
                        pltpu.VMEM((n_exp, LANES), F32),
                        pltpu.VMEM((SUBLANES, rows), I32),
                        pltpu.VMEM((SUBLANES, rows), I32),
                        pltpu.SMEM((SUBLANES, rows), I32),
                        pltpu.SMEM((SUBLANES, rows), I32),
                        pltpu.VMEM((rows, half), U32),
                        pltpu.VMEM((rows, half), U32),
                        pltpu.VMEM((d, de), F32),
                        pltpu.VMEM((d, de), F32),
                        pltpu.VMEM((de, d), F32),
                        pltpu.VMEM((d, de), BF16),
                        pltpu.VMEM((d, de), BF16),
                        pltpu.VMEM((de, d), BF16),
                        pltpu.SemaphoreType.DMA((2,)),
                        pltpu.SemaphoreType.DMA((2,)),
                        pltpu.SemaphoreType.DMA((2,)),
                        pltpu.SemaphoreType.DMA((3,)),
                        pltpu.SemaphoreType.REGULAR((1,))])
    return pl.pallas_call(
        functools.partial(_expert_kernel, rows=rows, n_tokens=t),
        grid_spec=grid_spec,
        out_shape=jax.ShapeDtypeStruct((n_blocks * rows, half), U32),
        compiler_params=_params("arbitrary"),
        name="routed_experts",
    )(block_e, first, next_e, first_row, n_used, cum3, ends, fpk, w_gate, w_up, w_down)


def _shared_kernel(fpk_ref, wg_ref, wu_ref, wd_ref, o_ref):
    half = fpk_ref.shape[1]
    lo, hi = _unpack_bf16_pair(fpk_ref[...])
    x_lo = lo.astype(BF16)
    x_hi = hi.astype(BF16)
    g = (jnp.dot(x_lo, wg_ref[0:half, :], preferred_element_type=F32)
         + jnp.dot(x_hi, wg_ref[half:2 * half, :], preferred_element_type=F32))
    u = (jnp.dot(x_lo, wu_ref[0:half, :], preferred_element_type=F32)
         + jnp.dot(x_hi, wu_ref[half:2 * half, :], preferred_element_type=F32))
    act = (_silu(g) * u).astype(BF16)
    o_ref[...] = jnp.dot(act, wd_ref[...], preferred_element_type=F32).astype(o_ref.dtype)


def _shared_expert(fpk, wg_bf, wu_bf, wd_bf):
    t, half = fpk.shape
    d = 2 * half
    ds = wg_bf.shape[1]
    tm = min(512, t)
    return pl.pallas_call(
        _shared_kernel,
        grid=(t // tm,),
        in_specs=[pl.BlockSpec((tm, half), lambda i: (i, 0)),
                  pl.BlockSpec((d, ds), lambda i: (0, 0)),
                  pl.BlockSpec((d, ds), lambda i: (0, 0)),
                  pl.BlockSpec((ds, d), lambda i: (0, 0))],
        out_specs=pl.BlockSpec((tm, d), lambda i: (i, 0)),
        out_shape=jax.ShapeDtypeStruct((t, d), BF16),
        compiler_params=_params("arbitrary"),
        name="shared_expert",
    )(fpk, wg_bf, wu_bf, wd_bf)


def _combine_kernel(dest_hbm, ys_hbm, wts_ref, sh_ref, h_ref, mod_ref, g_ref, o_ref,
                    dest_s0, dest_s1, gbuf0, gbuf1, sem_d, sem_g, *, tm, top_k, n_tiles):
    i = pl.program_id(0)
    n_rows = top_k * tm
    half = gbuf0.shape[1]
    dest_s, gbuf = (dest_s0, dest_s1), (gbuf0, gbuf1)
    last = n_tiles - 1

    def dest_copy(tile, s):
        return pltpu.make_async_copy(dest_hbm.at[tile], dest_s[s], sem_d.at[s])

    def gather_row(s, r, priority):
        row = dest_s[s][r]
        pltpu.async_copy(ys_hbm.at[pl.ds(row, 1), :], gbuf[s].at[pl.ds(r, 1), :], sem_g.at[s],
                         priority=priority)

    def gather_start(s, unrolled):
        if unrolled:
            for r in range(n_rows):
                gather_row(s, r, r % 2)
        else:
            def body(r, c):
                gather_row(s, r, 0)
                return c
            lax.fori_loop(0, n_rows, body, 0, unroll=8)

    def gather_wait(s):
        pltpu.make_async_copy(ys_hbm.at[pl.ds(0, n_rows), :], gbuf[s], sem_g.at[s]).wait()

    @pl.when(i == 0)
    def _():
        first_dest = dest_copy(0, 0)
        first_dest.start()
        first_dest.wait()
        gather_start(0, unrolled=False)
        dest_copy(jnp.minimum(1, last), 1).start()

    def run_tile(cur):
        nxt = 1 - cur
        gather_wait(cur)
        dest_copy(jnp.minimum(i + 1, last), nxt).wait()
        gather_start(nxt, unrolled=True)
        dest_copy(jnp.minimum(i + 2, last), cur).start()
        acc_lo = jnp.zeros((tm, half), F32)
        acc_hi = jnp.zeros((tm, half), F32)
        for k in range(top_k):
            lo, hi = _unpack_bf16_pair(gbuf[cur][k * tm:(k + 1) * tm, :])
            w = wts_ref[:, k:k + 1]
            acc_lo = acc_lo + w * lo
            acc_hi = acc_hi + w * hi
        moe_lo = acc_lo + sh_ref[:, 0:half].astype(F32)
        moe_hi = acc_hi + sh_ref[:, half:2 * half].astype(F32)
        ssq = (jnp.sum(moe_lo * moe_lo, axis=1, keepdims=True)
               + jnp.sum(moe_hi * moe_hi, axis=1, keepdims=True))
        r = lax.rsqrt(ssq * (1.0 / (2 * half)) + NORM_EPS)
        gate2 = mod_ref[0, 5:6, :]
        g = g_ref[...]
        o_ref[:, 0:half] = h_ref[:, 0:half] + gate2[:, 0:half] * (moe_lo * r * g[:, 0:half])
        o_ref[:, half:2 * half] = (h_ref[:, half:2 * half]
                                   + gate2[:, half:2 * half] * (moe_hi * r * g[:, half:2 * half]))

        @pl.when(i == last)
        def _():
            gather_wait(nxt)
            dest_copy(last, cur).wait()

    @pl.when(i % 2 == 0)
    def _():
        run_tile(0)

    @pl.when(i % 2 == 1)
    def _():
        run_tile(1)


def _combine(ys, dest_tiles, wts, shared, h, mod, g_post, seq, top_k):
    t, d = h.shape
    tm = COMBINE_TOKENS
    n_tiles = t // tm
    tiles_per_seq = seq // tm
    row = pl.BlockSpec((tm, d), lambda i: (i, 0))
    any_spec = pl.BlockSpec(memory_space=pl.ANY)
    return pl.pallas_call(
        functools.partial(_combine_kernel, tm=tm, top_k=top_k, n_tiles=n_tiles),
        grid=(n_tiles,),
        in_specs=[any_spec, any_spec,
                  pl.BlockSpec((tm, top_k), lambda i: (i, 0)),
                  row, row,
                  pl.BlockSpec((1, N_MOD, d), lambda i: (i // tiles_per_seq, 0, 0)),
                  pl.BlockSpec((1, d), lambda i: (0, 0))],
        out_specs=row,
        out_shape=jax.ShapeDtypeStruct((t, d), F32),
        scratch_shapes=[pltpu.SMEM((top_k * tm,), I32),
                        pltpu.SMEM((top_k * tm,), I32),
                        pltpu.VMEM((top_k * tm, d // 2), U32),
                        pltpu.VMEM((top_k * tm, d // 2), U32),
                        pltpu.SemaphoreType.DMA((2,)),
                        pltpu.SemaphoreType.DMA((2,))],
        compiler_params=_params("arbitrary"),
        name="combine",
    )(dest_tiles, ys, wts, shared, h, mod, g_post.reshape(1, d))


def _rope_tables(seq):
    half = HEAD_DIM // 2
    inv_freq = jnp.exp(-math.log(ROPE_THETA) * jnp.arange(half, dtype=F32) / half)
    ang = jnp.arange(seq, dtype=F32)[:, None] * inv_freq[None, :]
    cos, sin = jnp.cos(ang), jnp.sin(ang)
    return jnp.concatenate([cos, cos], axis=1), jnp.concatenate([-sin, sin], axis=1)


def kernel(x, c, w_mod, b_mod, g_pre_mix, g_post_mix, g_pre_ffn, g_post_ffn, w_in, w_pool, pool_scale, g_attn_out, g_pool_out, w_out, w_router, router_bias, w_gate, w_up, w_down, w_shared_gate, w_shared_up, w_shared_down):
    batch, seq, d = x.shape
    depth = w_mod.shape[0]
    n_groups, gd = w_pool.shape[1], w_pool.shape[2]
    pool_width = n_groups * gd
    attn_width = (w_in.shape[2] - pool_width) // 3
    n_heads = attn_width // HEAD_DIM
    assert n_groups == len(POOL_WINDOWS) and seq % (ATTN_BLOCK * DILATIONS[-1]) == 0
    cosf, sinf = _rope_tables(seq)

    h = x.reshape(batch * seq, d)
    for layer in range(depth):
        mod = _modulation(c, w_mod[layer], b_mod[layer])
        proj = _in_projection(h, g_pre_mix[layer], mod, w_in[layer].astype(BF16), cosf, sinf,
                              seq, attn_width)
        attn = _dilated_attention(proj, batch, seq, n_heads)
        mixed = _mixer_norms(attn, proj, w_pool[layer].astype(BF16), pool_scale[layer],
                             g_attn_out[layer], g_pool_out[layer], batch, seq)
        y = _matmul(mixed, w_out[layer].astype(BF16), F32)
        h, fpk, idx_t, wts_t, rank_t, cum = _post_mix_route(
            y, h, mod, g_post_mix[layer], g_pre_ffn[layer], w_router[layer], router_bias[layer], seq)
        block_tables, dest_tiles = _dispatch_tables(idx_t, rank_t, cum, MOE_ROWS, COMBINE_TOKENS)
        ys = _routed_experts(fpk, cum, block_tables, w_gate[layer], w_up[layer], w_down[layer])
        shared = _shared_expert(fpk, w_shared_gate[layer].astype(BF16),
                                w_shared_up[layer].astype(BF16), w_shared_down[layer].astype(BF16))
        h = _combine(ys, dest_tiles, wts_t.T, shared, h, mod, g_post_ffn[layer], seq, TOP_K)
    return h.reshape(batch, seq, d)
```

```python
import functools
import math

import jax
import jax.numpy as jnp
from jax import lax
from jax.experimental import pallas as pl
from jax.experimental.pallas import tpu as pltpu

F32 = jnp.float32
BF16 = jnp.bfloat16
I32 = jnp.int32
U32 = jnp.uint32

HEAD_DIM = 128
ATTN_BLOCK = 128
DILATIONS = (1, 4, 16)
POOL_WINDOWS = (2, 4, 8, 16)
POOL_HALO = 16
ROPE_THETA = 10000.0
TOP_K = 8
N_EXPERT_GROUPS = 8
TOPK_GROUPS = 4
ROUTED_SCALE = 2.5
N_MOD = 6
NORM_EPS = 1e-6
MASK_VALUE = -1e30

LANES = 128
SUBLANES = 8
VMEM_LIMIT_BYTES = 60 * 1024 * 1024

MOE_ROWS = 256
EXPERT_LOOKAHEAD = 2
COMBINE_TOKENS = 128

_NT = (((1,), (1,)), ((), ()))


def _params(*sem):
    return pltpu.CompilerParams(dimension_semantics=sem, vmem_limit_bytes=VMEM_LIMIT_BYTES)


def _rms(x, g):
    return x * lax.rsqrt(jnp.mean(x * x, axis=-1, keepdims=True) + NORM_EPS) * g


def _silu(x):
    return x * jax.nn.sigmoid(x)


def _pack_bf16_pair(x):
    n = x.shape[1] // 2
    lo = lax.bitcast_convert_type(x[:, :n].astype(BF16).astype(F32), U32)
    hi = lax.bitcast_convert_type(x[:, n:].astype(BF16).astype(F32), U32)
    return (lo >> 16) | hi


def _unpack_bf16_pair(w):
    lo = lax.bitcast_convert_type(w << 16, F32)
    hi = lax.bitcast_convert_type(w & jnp.uint32(0xFFFF0000), F32)
    return lo, hi


def _mod_kernel(c_ref, w_ref, b_ref, o_ref):
    cond = _silu(c_ref[...])
    o_ref[...] = jnp.dot(cond.astype(BF16), w_ref[...].astype(BF16),
                         preferred_element_type=F32) + b_ref[...]


def _modulation(c, w_mod, b_mod):
    b, d = c.shape
    n = w_mod.shape[1]
    bp = -(-b // SUBLANES) * SUBLANES
    tn = 512
    out = pl.pallas_call(
        _mod_kernel,
        grid=(n // tn,),
        in_specs=[pl.BlockSpec((bp, d), lambda j: (0, 0)),
                  pl.BlockSpec((d, tn), lambda j: (0, j)),
                  pl.BlockSpec((1, tn), lambda j: (0, j))],
        out_specs=pl.BlockSpec((bp, tn), lambda j: (0, j)),
        out_shape=jax.ShapeDtypeStruct((bp, n), F32),
        compiler_params=_params("arbitrary"),
        name="modulation",
    )(jnp.pad(c, ((0, bp - b), (0, 0))), w_mod, b_mod.reshape(1, n))
    return out[:b].reshape(b, N_MOD, d)


def _inproj_kernel(x_ref, g_ref, mod_ref, w_ref, cos_ref, sin_ref, o_ref, a_scr, *,
                   n_q_tiles, n_rope_tiles):
    j = pl.program_id(1)

    @pl.when(j == 0)
    def _():
        shift1 = mod_ref[0, 0:1, :]
        scale1 = mod_ref[0, 1:2, :]
        a = _rms(x_ref[...], g_ref[...]) * (1.0 + scale1) + shift1
        a_scr[...] = a.astype(BF16)

    acc = jnp.dot(a_scr[...], w_ref[...], preferred_element_type=F32)

    @pl.when(j < n_rope_tiles)
    def _():
        qk_scale = jnp.where(j < n_q_tiles, HEAD_DIM ** -0.5, 1.0)
        cosf = cos_ref[...] * qk_scale
        sinf = sin_ref[...] * qk_scale
        for hh in range(acc.shape[1] // HEAD_DIM):
            blk = acc[:, hh * HEAD_DIM:(hh + 1) * HEAD_DIM]
            rot = blk * cosf + pltpu.roll(blk, HEAD_DIM // 2, 1) * sinf
            o_ref[:, hh * HEAD_DIM:(hh + 1) * HEAD_DIM] = rot.astype(o_ref.dtype)

    @pl.when(j >= n_rope_tiles)
    def _():
        o_ref[...] = acc.astype(o_ref.dtype)


def _in_projection(x2, g, mod, w_in_bf, cosf, sinf, seq, attn_width):
    t, d = x2.shape
    n = w_in_bf.shape[1]
    tm = min(512, seq)
    tn = min(1024, attn_width)
    tiles_per_seq = seq // tm
    return pl.pallas_call(
        functools.partial(_inproj_kernel, n_q_tiles=attn_width // tn, n_rope_tiles=2 * attn_width // tn),
        grid=(t // tm, n // tn),
        in_specs=[pl.BlockSpec((tm, d), lambda i, j: (i, 0)),
                  pl.BlockSpec((1, d), lambda i, j: (0, 0)),
                  pl.BlockSpec((1, N_MOD, d), lambda i, j: (i // tiles_per_seq, 0, 0)),
                  pl.BlockSpec((d, tn), lambda i, j: (0, j)),
                  pl.BlockSpec((tm, HEAD_DIM), lambda i, j: (i % tiles_per_seq, 0)),
                  pl.BlockSpec((tm, HEAD_DIM), lambda i, j: (i % tiles_per_seq, 0))],
        out_specs=pl.BlockSpec((tm, tn), lambda i, j: (i, j)),
        out_shape=jax.ShapeDtypeStruct((t, n), BF16),
        scratch_shapes=[pltpu.VMEM((tm, d), BF16)],
        compiler_params=_params("arbitrary", "arbitrary"),
        name="in_projection",
    )(x2, g.reshape(1, d), mod, w_in_bf, cosf, sinf)


def _attn_kernel(q_ref, k_ref, v_ref, o_ref, qf, kf, vf, qs, ks, vs, accs, maxs, dens, stage, *, seq):
    blk = ATTN_BLOCK
    n_blk = seq // blk
    qf[...] = q_ref[...].astype(F32)
    kf[...] = k_ref[...].astype(F32)
    vf[...] = v_ref[...].astype(F32)
    vs[:, HEAD_DIM:2 * HEAD_DIM] = jnp.ones((vs.shape[0], HEAD_DIM), BF16)

    qi = lax.broadcasted_iota(I32, (blk, blk), 0)
    ki = lax.broadcasted_iota(I32, (blk, blk), 1)
    causal_bias = jnp.where(ki <= qi, 0.0, MASK_VALUE)
    upper_bias = jnp.where(ki >= qi, 0.0, MASK_VALUE)

    for bi, dil in enumerate(DILATIONS):
        sub_len = seq // dil
        nbk = sub_len // blk
        lead = blk if nbk > 1 else 0
        pitch = sub_len + lead
        def to_sub_order(res, carry, dil=dil, sub_len=sub_len, lead=lead, pitch=pitch):
            base = pl.multiple_of(jnp.asarray(res * pitch, I32), blk)
            q0 = pl.multiple_of(jnp.asarray(res * sub_len, I32), blk)
            if dil == 1:
                q_src, k_src, v_src = q_ref[...], k_ref[...], v_ref[...]
            else:
                q_src = qf[pl.ds(res, sub_len, stride=dil), :].astype(BF16)
                k_src = kf[pl.ds(res, sub_len, stride=dil), :].astype(BF16)
                v_src = vf[pl.ds(res, sub_len, stride=dil), :].astype(BF16)
            qs[pl.ds(q0, sub_len), :] = q_src
            if lead:
                ks[pl.ds(base, lead), :] = jnp.zeros((lead, HEAD_DIM), BF16)
                vs[pl.ds(base, lead), 0:HEAD_DIM] = jnp.zeros((lead, HEAD_DIM), BF16)
            ks[pl.ds(base + lead, sub_len), :] = k_src
            vs[pl.ds(base + lead, sub_len), 0:HEAD_DIM] = v_src
            return carry

        if dil == 1:
            to_sub_order(0, 0)
        else:
            lax.fori_loop(0, dil, to_sub_order, 0)

        def blocks(ref, offset, dil=dil, pitch=pitch, sub_len=sub_len, nbk=nbk):
            parts = [ref[res * pitch + offset:res * pitch + offset + sub_len, :]
                     .reshape(nbk, blk, ref.shape[1]) for res in range(dil)]
            return parts[0] if dil == 1 else jnp.concatenate(parts, axis=0)

        q3 = qs[...].reshape(n_blk, blk, HEAD_DIM)
        s_own = jnp.einsum('nqc,nkc->nqk', q3, blocks(ks, lead),
                           preferred_element_type=F32) + causal_bias[None]
        if nbk > 1:
            first = (lax.broadcasted_iota(I32, (n_blk, blk, blk), 0) & (nbk - 1)) == 0
            prev_bias = jnp.where(first, MASK_VALUE, upper_bias[None])
            s_prev = jnp.einsum('nqc,nkc->nqk', q3, blocks(ks, 0),
                                preferred_element_type=F32) + prev_bias
            m = jnp.max(jnp.maximum(s_own, s_prev), axis=-1, keepdims=True)
            accx = (jnp.einsum('nqk,nkc->nqc', jnp.exp(s_own - m).astype(BF16), blocks(vs, lead),
                               preferred_element_type=F32)
                    + jnp.einsum('nqk,nkc->nqc', jnp.exp(s_prev - m).astype(BF16), blocks(vs, 0),
                                 preferred_element_type=F32))
        else:
            m = jnp.max(s_own, axis=-1, keepdims=True)
            accx = jnp.einsum('nqk,nkc->nqc', jnp.exp(s_own - m).astype(BF16), blocks(vs, lead),
                              preferred_element_type=F32)
        m_b = jnp.broadcast_to(m, (n_blk, blk, HEAD_DIM))
        results = (accx[:, :, 0:HEAD_DIM], accx[:, :, HEAD_DIM:2 * HEAD_DIM], m_b)
        if dil == 1:
            for dst, val in zip((accs, dens, maxs), results):
                dst[bi] = val.reshape(seq, HEAD_DIM)
        else:
            for j, val in enumerate(results):
                stage[j] = val.reshape(seq, HEAD_DIM)

            def to_natural_order(res, carry, dil=dil, sub_len=sub_len, bi=bi):
                r0 = pl.multiple_of(jnp.asarray(res * sub_len, I32), blk)
                for j, dst in enumerate((accs, dens, maxs)):
                    dst[bi, pl.ds(res, sub_len, stride=dil), :] = stage[j, pl.ds(r0, sub_len), :]
                return carry

            lax.fori_loop(0, dil, to_natural_order, 0)

    def merge(c, carry):
        r0 = pl.multiple_of(c * blk, blk)
        ms = [maxs[bi, pl.ds(r0, blk), :] for bi in range(len(DILATIONS))]
        m_all = functools.reduce(jnp.maximum, ms)
        ws = [jnp.exp(m - m_all) for m in ms]
        den = sum(w * dens[bi, pl.ds(r0, blk), :] for bi, w in enumerate(ws))
        num = sum(w * accs[bi, pl.ds(r0, blk), :] for bi, w in enumerate(ws))
        o_ref[pl.ds(r0, blk), :] = (num / den).astype(o_ref.dtype)
        return carry

    lax.fori_loop(0, n_blk, merge, 0, unroll=2)


def _dilated_attention(proj, batch, seq, n_heads):
    t = proj.shape[0]
    nb = len(DILATIONS)
    key_rows = max(d * (seq // d + (ATTN_BLOCK if seq // d > ATTN_BLOCK else 0)) for d in DILATIONS)
    return pl.pallas_call(
        functools.partial(_attn_kernel, seq=seq),
        grid=(batch, n_heads),
        in_specs=[pl.BlockSpec((seq, HEAD_DIM), lambda b, h: (b, h)),
                  pl.BlockSpec((seq, HEAD_DIM), lambda b, h: (b, n_heads + h)),
                  pl.BlockSpec((seq, HEAD_DIM), lambda b, h: (b, 2 * n_heads + h))],
        out_specs=pl.BlockSpec((seq, HEAD_DIM), lambda b, h: (b, h)),
        out_shape=jax.ShapeDtypeStruct((t, n_heads * HEAD_DIM), BF16),
        scratch_shapes=[pltpu.VMEM((seq, HEAD_DIM), F32),
                        pltpu.VMEM((seq, HEAD_DIM), F32),
                        pltpu.VMEM((seq, HEAD_DIM), F32),
                        pltpu.VMEM((seq, HEAD_DIM), BF16),
                        pltpu.VMEM((key_rows, HEAD_DIM), BF16),
                        pltpu.VMEM((key_rows, 2 * HEAD_DIM), BF16),
                        pltpu.VMEM((nb, seq, HEAD_DIM), F32),
                        pltpu.VMEM((nb, seq, HEAD_DIM), F32),
                        pltpu.VMEM((nb, seq, HEAD_DIM), F32),
                        pltpu.VMEM((3, seq, HEAD_DIM), F32)],
        compiler_params=_params("arbitrary", "arbitrary"),
        name="dilated_attention",
    )(proj, proj, proj)


def _mix_kernel(attn_ref, u_ref, wp_ref, ps_ref, ga_ref, gp_ref, o_ref, ext, *, tr, aw, gd):
    j = pl.program_id(1)
    n_groups = len(POOL_WINDOWS)

    @pl.when(j == 0)
    def _():
        ext[0:POOL_HALO, :] = jnp.zeros((POOL_HALO, ext.shape[1]), BF16)

    @pl.when(j > 0)
    def _():
        ext[0:POOL_HALO, :] = ext[tr:tr + POOL_HALO, :]

    ext[POOL_HALO:POOL_HALO + tr, :] = u_ref[...]

    o_ref[:, 0:aw] = _rms(attn_ref[...].astype(F32), ga_ref[...]).astype(o_ref.dtype)

    pos = j * tr + lax.broadcasted_iota(I32, (tr, 1), 0)
    diff = (lax.broadcasted_iota(I32, (tr, tr + POOL_HALO), 0) + POOL_HALO
            - lax.broadcasted_iota(I32, (tr, tr + POOL_HALO), 1))
    ys = []
    ssq = jnp.zeros((tr, 1), F32)
    for g, w in enumerate(POOL_WINDOWS):
        band = ((diff >= 0) & (diff < w)).astype(BF16)
        win = jnp.dot(band, ext[:, g * gd:(g + 1) * gd], preferred_element_type=F32)
        cnt = jnp.minimum(pos + 1, w).astype(F32)
        mix = win * (1.0 / cnt) - u_ref[:, g * gd:(g + 1) * gd].astype(F32)
        y = jnp.dot(mix.astype(BF16), wp_ref[g], preferred_element_type=F32) * ps_ref[g:g + 1, :]
        ys.append(y)
        ssq = ssq + jnp.sum(y * y, axis=1, keepdims=True)
    r = lax.rsqrt(ssq * (1.0 / (n_groups * gd)) + NORM_EPS)
    for g in range(n_groups):
        o_ref[:, aw + g * gd:aw + (g + 1) * gd] = (
            ys[g] * r * gp_ref[:, g * gd:(g + 1) * gd]).astype(o_ref.dtype)


def _mixer_norms(attn, proj, w_pool_bf, pool_scale, g_attn, g_pool, batch, seq):
    t, aw = attn.shape
    n_groups, gd, _ = w_pool_bf.shape
    pw = n_groups * gd
    tr = min(256, seq)
    tiles = seq // tr
    u_col_block = (proj.shape[1] - pw) // pw
    return pl.pallas_call(
        functools.partial(_mix_kernel, tr=tr, aw=aw, gd=gd),
        grid=(batch, tiles),
        in_specs=[pl.BlockSpec((tr, aw), lambda b, j: (b * tiles + j, 0)),
                  pl.BlockSpec((tr, pw), lambda b, j: (b * tiles + j, u_col_block)),
                  pl.BlockSpec((n_groups, gd, gd), lambda b, j: (0, 0, 0)),
                  pl.BlockSpec((n_groups, gd), lambda b, j: (0, 0)),
                  pl.BlockSpec((1, aw), lambda b, j: (0, 0)),
                  pl.BlockSpec((1, pw), lambda b, j: (0, 0))],
        out_specs=pl.BlockSpec((tr, aw + pw), lambda b, j: (b * tiles + j, 0)),
        out_shape=jax.ShapeDtypeStruct((t, aw + pw), BF16),
        scratch_shapes=[pltpu.VMEM((tr + POOL_HALO, pw), BF16)],
        compiler_params=_params("arbitrary", "arbitrary"),
        name="mixer_norms",
    )(attn, proj, w_pool_bf, pool_scale, g_attn.reshape(1, aw), g_pool.reshape(1, pw))


def _matmul_kernel(a_ref, b_ref, o_ref):
    o_ref[...] = jnp.dot(a_ref[...], b_ref[...], preferred_element_type=F32).astype(o_ref.dtype)


def _matmul(a, b, out_dtype, tm=1024, tn=1024):
    m, k = a.shape
    n = b.shape[1]
    tm, tn = min(tm, m), min(tn, n)
    return pl.pallas_call(
        _matmul_kernel,
        grid=(m // tm, n // tn),
        in_specs=[pl.BlockSpec((tm, k), lambda i, j: (i, 0)),
                  pl.BlockSpec((k, tn), lambda i, j: (0, j))],
        out_specs=pl.BlockSpec((tm, tn), lambda i, j: (i, j)),
        out_shape=jax.ShapeDtypeStruct((m, n), out_dtype),
        compiler_params=_params("arbitrary", "arbitrary"),
        name="out_projection",
    )(a, b)


def _split_bf16(x):
    hi = x.astype(BF16)
    lo = (x - hi.astype(F32)).astype(BF16)
    return hi, lo


def _route(f, w_t, bias):
    f_hi, f_lo = _split_bf16(f)
    w_hi, w_lo = _split_bf16(w_t)
    logits = (lax.dot_general(w_hi, f_hi, _NT, preferred_element_type=F32)
              + lax.dot_general(w_hi, f_lo, _NT, preferred_element_type=F32)
              + lax.dot_general(w_lo, f_hi, _NT, preferred_element_type=F32))
    n_exp, tm = logits.shape
    per = n_exp // N_EXPERT_GROUPS
    scores = jax.nn.sigmoid(logits)
    biased = scores + bias
    ninf = -jnp.inf

    b3 = biased.reshape(N_EXPERT_GROUPS, per, tm)
    memb = lax.broadcasted_iota(I32, b3.shape, 1)
    m1 = jnp.max(b3, axis=1, keepdims=True)
    i1 = jnp.min(jnp.where(b3 == m1, memb, per), axis=1, keepdims=True)
    m2 = jnp.max(jnp.where(memb == i1, ninf, b3), axis=1, keepdims=True)
    gscore = (m1 + m2).reshape(N_EXPERT_GROUPS, tm)

    gid = lax.broadcasted_iota(I32, gscore.shape, 0)
    gmask = jnp.zeros(gscore.shape, jnp.bool_)
    cur = gscore
    for _ in range(TOPK_GROUPS):
        mx = jnp.max(cur, axis=0, keepdims=True)
        ix = jnp.min(jnp.where(cur == mx, gid, N_EXPERT_GROUPS), axis=0, keepdims=True)
        sel = gid == ix
        gmask = gmask | sel
        cur = jnp.where(sel, ninf, cur)
    emask = jnp.broadcast_to(gmask.reshape(N_EXPERT_GROUPS, 1, tm), b3.shape).reshape(n_exp, tm)

    masked = jnp.where(emask, biased, ninf)
    eid = lax.broadcasted_iota(I32, masked.shape, 0)
    idx_rows, w_rows, sels = [], [], []
    chosen = jnp.zeros(masked.shape, F32)
    for _ in range(TOP_K):
        mx = jnp.max(masked, axis=0, keepdims=True)
        ix = jnp.min(jnp.where(masked == mx, eid, n_exp), axis=0, keepdims=True)
        sel = eid == ix
        idx_rows.append(ix)
        w_rows.append(jnp.sum(jnp.where(sel, scores, 0.0), axis=0, keepdims=True))
        sels.append(sel)
        chosen = jnp.where(sel, 1.0, chosen)
        masked = jnp.where(sel, ninf, masked)
    wsel = jnp.concatenate(w_rows, axis=0)
    wts = wsel / jnp.sum(wsel, axis=0, keepdims=True) * ROUTED_SCALE
    return jnp.concatenate(idx_rows, axis=0), wts, sels, chosen


def _postmix_kernel(y_ref, x_ref, mod_ref, gpost_ref, gpre_ref, wt_ref, bias_ref,
                    h_ref, fpk_ref, idx_ref, wts_ref, rank_ref, cum_ref, carry):
    i = pl.program_id(0)

    @pl.when(i == 0)
    def _():
        carry[...] = jnp.zeros(carry.shape, F32)

    gate1 = mod_ref[0, 2:3, :]
    shift2 = mod_ref[0, 3:4, :]
    scale2 = mod_ref[0, 4:5, :]
    h = x_ref[...] + gate1 * _rms(y_ref[...], gpost_ref[...])
    h_ref[...] = h
    f = _rms(h, gpre_ref[...]) * (1.0 + scale2) + shift2
    fpk_ref[...] = _pack_bf16_pair(f)

    idx, wts, sels, chosen = _route(f, wt_ref[...], bias_ref[...])
    tm = f.shape[0]
    upper = (lax.broadcasted_iota(I32, (tm, tm), 0)
             <= lax.broadcasted_iota(I32, (tm, tm), 1)).astype(BF16)
    cum = jnp.dot(chosen.astype(BF16), upper, preferred_element_type=F32) + carry[...]
    carry[...] = cum[:, tm - 1:tm]
    idx_ref[...] = idx
    wts_ref[...] = wts
    cum_ref[...] = cum
    rank_ref[...] = jnp.concatenate(
        [jnp.sum(jnp.where(sel, cum, 0.0), axis=0, keepdims=True) for sel in sels],
        axis=0).astype(I32) - 1


def _post_mix_route(y, x2, mod, g_post, g_pre, w_router, router_bias, seq):
    t, d = x2.shape
    n_exp = w_router.shape[1]
    tm = min(256, seq)
    tiles_per_seq = seq // tm
    row = pl.BlockSpec((tm, d), lambda i: (i, 0))
    vec = pl.BlockSpec((1, d), lambda i: (0, 0))
    kt = pl.BlockSpec((TOP_K, tm), lambda i: (0, i))
    return pl.pallas_call(
        _postmix_kernel,
        grid=(t // tm,),
        in_specs=[row, row,
                  pl.BlockSpec((1, N_MOD, d), lambda i: (i // tiles_per_seq, 0, 0)),
                  vec, vec,
                  pl.BlockSpec((n_exp, d), lambda i: (0, 0)),
                  pl.BlockSpec((n_exp, 1), lambda i: (0, 0))],
        out_specs=[row, pl.BlockSpec((tm, d // 2), lambda i: (i, 0)), kt, kt, kt,
                   pl.BlockSpec((n_exp, tm), lambda i: (0, i))],
        out_shape=[jax.ShapeDtypeStruct((t, d), F32),
                   jax.ShapeDtypeStruct((t, d // 2), U32),
                   jax.ShapeDtypeStruct((TOP_K, t), I32),
                   jax.ShapeDtypeStruct((TOP_K, t), F32),
                   jax.ShapeDtypeStruct((TOP_K, t), I32),
                   jax.ShapeDtypeStruct((n_exp, t), F32)],
        scratch_shapes=[pltpu.VMEM((n_exp, 1), F32)],
        compiler_params=_params("arbitrary"),
        name="post_mix_route",
    )(y, x2, mod, g_post.reshape(1, d), g_pre.reshape(1, d), w_router.T, router_bias.reshape(n_exp, 1))


def _dispatch_tables(idx_t, rank_t, cum, rows_per_block, tokens_per_tile):
    k, t = idx_t.shape
    n_exp = cum.shape[0]
    n = k * t
    n_blocks = n // rows_per_block + n_exp
    experts = jnp.arange(n_exp, dtype=I32)
    counts = cum[:, -1].astype(I32)
    padded = (counts + rows_per_block - 1) // rows_per_block * rows_per_block
    pad_ends = jnp.cumsum(padded)
    pad_starts = pad_ends - padded
    n_used = (pad_ends[-1] // rows_per_block).astype(I32)
    n_tab = n_blocks + EXPERT_LOOKAHEAD
    blk_start = jnp.arange(n_tab, dtype=I32) * rows_per_block
    block_e = jnp.minimum(jnp.sum((pad_ends[None, :] <= blk_start[:, None]).astype(I32), axis=1),
                          n_exp - 1).astype(I32)
    of_block = block_e[:, None] == experts[None, :]
    first_row = blk_start - jnp.sum(jnp.where(of_block, pad_starts[None, :], 0), axis=1)
    prev_e = jnp.concatenate([jnp.full((1,), -1, I32), block_e[:-1]])
    first = (block_e != prev_e).astype(I32)
    after = jnp.sum(jnp.where(of_block, pad_ends[None, :], 0), axis=1) // rows_per_block
    next_e = jnp.where(after < n_used, block_e[jnp.minimum(after, n_tab - 1)], -1).astype(I32)
    dest = jnp.sum(jnp.where(idx_t[:, :, None] == experts[None, None, :],
                             pad_starts[None, None, :], 0), axis=2) + rank_t
    n_tiles = t // tokens_per_tile
    dest_tiles = dest.reshape(k, n_tiles, tokens_per_tile).transpose(1, 0, 2).reshape(
        n_tiles, k * tokens_per_tile).astype(I32)
    return (block_e, first, next_e, first_row.astype(I32), n_used.reshape(1)), dest_tiles


def _dispatch_kernel(be_ref, row0_ref, nused_ref,
                     cum_hbm, ends_hbm, fpk_hbm,
                     xs_hbm,
                     cum_v, ends_v, tok_v0, tok_v1, tok_s0, tok_s1, xs0, xs1, zero_v,
                     sem_cum, sem_tok, sem_x, sem_o, *, rows, n_tokens):
    i = pl.program_id(0)
    n_used = nused_ref[0]
    n_buckets = cum_v.shape[1]
    tok_v, tok_s, xs = (tok_v0, tok_v1), (tok_s0, tok_s1), (xs0, xs1)

    def out_copy(src, blk, slot):
        return pltpu.make_async_copy(src, xs_hbm.at[pl.ds(pl.multiple_of(blk * rows, rows), rows), :],
                                     sem_o.at[slot])

    def tok_copy(slot):
        return pltpu.make_async_copy(tok_v[slot], tok_s[slot], sem_tok.at[slot])

    def invert(blk, slot):
        e = be_ref[blk]
        row0 = row0_ref[blk]
        ends = ends_v[pl.ds(e, 1), :]
        counts = cum_v[e]
        c_hi = jnp.floor(counts * (1.0 / LANES))
        c_lo = counts - c_hi * LANES
        if n_buckets < LANES:
            fill = jnp.zeros((LANES - n_buckets, LANES), F32)
            c_hi = jnp.concatenate([c_hi, fill], axis=0)
            c_lo = jnp.concatenate([c_lo, fill], axis=0)
        c_hi = c_hi.astype(BF16)
        c_lo = c_lo.astype(BF16)
        ones = jnp.ones((SUBLANES, LANES), BF16)
        lane = lax.broadcasted_iota(I32, (LANES, LANES), 1)
        for c in range(rows // LANES):
            j = (row0 + c * LANES + lax.broadcasted_iota(I32, (LANES, 1), 0)).astype(F32)
            done = jnp.where(ends <= j, 1.0, 0.0)
            before = jnp.where(lane == 0, 1.0, pltpu.roll(done, 1, 1))
            pick = ((1.0 - done) * before).astype(BF16)
            in_bucket = (jnp.dot(pick, c_hi, preferred_element_type=F32) * LANES
                         + jnp.dot(pick, c_lo, preferred_element_type=F32))
            below = jnp.where(in_bucket <= j, 1.0, 0.0).astype(BF16)
            n_done = lax.dot_general(ones, done.astype(BF16), _NT, preferred_element_type=F32)
            n_below = lax.dot_general(ones, below, _NT, preferred_element_type=F32)
            tok = jnp.minimum(n_done * LANES + n_below, n_tokens - 1.0)
            tok_v[slot][:, c * LANES:(c + 1) * LANES] = tok.astype(I32)

    def gather_row(slot, r):
        tok = tok_s[slot][0, r]
        pltpu.make_async_copy(fpk_hbm.at[pl.ds(tok, 1), :], xs[slot].at[pl.ds(r, 1), :],
                              sem_x.at[slot]).start()

    def gather_start(slot, unrolled):
        if unrolled:
            for r in range(rows):
                gather_row(slot, r)
        else:
            def body(r, c):
                gather_row(slot, r)
                return c
            lax.fori_loop(0, rows, body, 0, unroll=8)

    def gather_wait(slot):
        pltpu.make_async_copy(fpk_hbm.at[pl.ds(0, rows), :], xs[slot], sem_x.at[slot]).wait()

    @pl.when(i == 0)
    def _():
        zero_v[...] = jnp.zeros(zero_v.shape, zero_v.dtype)

    @pl.when((i == 0) & (n_used > 0))
    def _():
        cum_copy = pltpu.make_async_copy(cum_hbm, cum_v, sem_cum.at[0])
        ends_copy = pltpu.make_async_copy(ends_hbm, ends_v, sem_cum.at[1])
        cum_copy.start()
        ends_copy.start()
        cum_copy.wait()
        ends_copy.wait()
        invert(0, 0)
        tok_copy(0).start()
        tok_copy(0).wait()
        gather_start(0, unrolled=False)
        invert(1, 1)
        tok_copy(1).start()

    def run_block(cur):
        nxt = 1 - cur
        gather_wait(cur)
        out_copy(xs[cur], i, cur).start()
        tok_copy(nxt).wait()

        @pl.when(i >= 1)
        def _():
            out_copy(xs[nxt], i - 1, nxt).wait()

        gather_start(nxt, unrolled=True)
        invert(i + 2, cur)
        tok_copy(cur).start()

        @pl.when(i == n_used - 1)
        def _():
            gather_wait(nxt)
            tok_copy(cur).wait()
            out_copy(xs[cur], i, cur).wait()

    @pl.when((i < n_used) & (i % 2 == 0))
    def _():
        run_block(0)

    @pl.when((i < n_used) & (i % 2 == 1))
    def _():
        run_block(1)

    @pl.when(i >= n_used)
    def _():
        tail = out_copy(zero_v, i, 0)
        tail.start()
        tail.wait()


def _dispatch_rows(fpk, cum, block_tables):
    block_e, _, _, first_row, n_used = block_tables
    t, half = fpk.shape
    n_exp = cum.shape[0]
    rows = MOE_ROWS
    n_blocks = block_e.shape[0] - EXPERT_LOOKAHEAD
    n_buckets = t // LANES
    assert n_buckets <= LANES, "two-level token search covers at most 128 buckets of 128 tokens"
    cum3 = cum.reshape(n_exp, n_buckets, LANES)
    ends = jnp.pad(cum3[:, :, LANES - 1], ((0, 0), (0, LANES - n_buckets)),
                   constant_values=float(2 * t))
    any_spec = pl.BlockSpec(memory_space=pl.ANY)
    grid_spec = pltpu.PrefetchScalarGridSpec(
        num_scalar_prefetch=3,
        grid=(n_blocks,),
        in_specs=[any_spec, any_spec, any_spec],
        out_specs=any_spec,
        scratch_shapes=[pltpu.VMEM((n_exp, n_buckets, LANES), F32),
                        pltpu.VMEM((n_exp, LANES), F32),
                        pltpu.VMEM((SUBLANES, rows), I32),
                        pltpu.VMEM((SUBLANES, rows), I32),
                        pltpu.SMEM((SUBLANES, rows), I32),
                        pltpu.SMEM((SUBLANES, rows), I32),
                        pltpu.VMEM((rows, half), U32),
                        pltpu.VMEM((rows, half), U32),
                        pltpu.VMEM((rows, half), U32),
                        pltpu.SemaphoreType.DMA((2,)),
                        pltpu.SemaphoreType.DMA((2,)),
                        pltpu.SemaphoreType.DMA((2,)),
                        pltpu.SemaphoreType.DMA((2,))])
    return pl.pallas_call(
        functools.partial(_dispatch_kernel, rows=rows, n_tokens=t),
        grid_spec=grid_spec,
        out_shape=jax.ShapeDtypeStruct((n_blocks * rows, half), U32),
        compiler_params=_params("arbitrary"),
        name="dispatch_rows",
    )(block_e, first_row, n_used, cum3, ends, fpk)


def _expert_kernel(be_ref, first_ref, nexte_ref, nused_ref,
                   x_ref, wg_hbm, wu_hbm, wd_hbm,
                   o_ref,
                   stage_g, stage_u, stage_d, wg, wu, wd, sem_w):
    i = pl.program_id(0)
    n_used = nused_ref[0]
    half = wd.shape[1] // 2

    def weight_pairs(e):
        return ((wg_hbm.at[e], stage_g), (wu_hbm.at[e], stage_u), (wd_hbm.at[e], stage_d))

    def weights_start(e):
        for k, (src, dst) in enumerate(weight_pairs(e)):
            pltpu.async_copy(src, dst, sem_w.at[k], priority=1)

    def weights_wait(e):
        for k, (src, dst) in enumerate(weight_pairs(e)):
            pltpu.make_async_copy(src, dst, sem_w.at[k]).wait()

    def cast_weights(src, dst):
        chunk = math.gcd(256, src.shape[0])
        def body(c, carry):
            r0 = pl.multiple_of(c * chunk, chunk)
            dst[pl.ds(r0, chunk), :] = src[pl.ds(r0, chunk), :].astype(BF16)
            return carry
        lax.fori_loop(0, src.shape[0] // chunk, body, 0)

    @pl.when((i == 0) & (n_used > 0))
    def _():
        weights_start(be_ref[0])

    @pl.when(i < n_used)
    def _():
        @pl.when(first_ref[i] == 1)
        def _():
            weights_wait(be_ref[i])
            cast_weights(stage_g, wg)
            cast_weights(stage_u, wu)
            cast_weights(stage_d, wd)

            @pl.when(nexte_ref[i] >= 0)
            def _():
                weights_start(nexte_ref[i])

        lo, hi = _unpack_bf16_pair(x_ref[...])
        x_lo = lo.astype(BF16)
        x_hi = hi.astype(BF16)
        g = (jnp.dot(x_lo, wg[0:half, :], preferred_element_type=F32)
             + jnp.dot(x_hi, wg[half:2 * half, :], preferred_element_type=F32))
        u = (jnp.dot(x_lo, wu[0:half, :], preferred_element_type=F32)
             + jnp.dot(x_hi, wu[half:2 * half, :], preferred_element_type=F32))
        act = (_silu(g) * u).astype(BF16)
        o_ref[...] = _pack_bf16_pair(jnp.dot(act, wd[...], preferred_element_type=F32))

    @pl.when(i >= n_used)
    def _():
        o_ref[...] = jnp.zeros(o_ref.shape, o_ref.dtype)


def _routed_experts(xs, block_tables, w_gate, w_up, w_down):
    block_e, first, next_e, _, n_used = block_tables
    n_rows, half = xs.shape
    d = 2 * half
    n_exp, _, de = w_gate.shape
    rows = MOE_ROWS
    n_blocks = n_rows // rows
    any_spec = pl.BlockSpec(memory_space=pl.ANY)
    grid_spec = pltpu.PrefetchScalarGridSpec(
        num_scalar_prefetch=4,
        grid=(n_blocks,),
        in_specs=[pl.BlockSpec((rows, half), lambda i, *_: (i, 0)), any_spec, any_spec, any_spec],
        out_specs=pl.BlockSpec((rows, half), lambda i, *_: (i, 0)),
        scratch_shapes=[pltpu.VMEM((d, de), F32),
                        pltpu.VMEM((d, de), F32),
                        pltpu.VMEM((de, d), F32),
                        pltpu.VMEM((d, de), BF16),
                        pltpu.VMEM((d, de), BF16),
                        pltpu.VMEM((de, d), BF16),
                        pltpu.SemaphoreType.DMA((3,))])
    return pl.pallas_call(
        _expert_kernel,
        grid_spec=grid_spec,
        out_shape=jax.ShapeDtypeStruct((n_rows, half), U32),
        compiler_params=_params("arbitrary"),
        name="routed_experts",
    )(block_e, first, next_e, n_used, xs, w_gate, w_up, w_down)


def _shared_kernel(fpk_ref, wg_ref, wu_ref, wd_ref, o_ref):
    half = fpk_ref.shape[1]
    lo, hi = _unpack_bf16_pair(fpk_ref[...])
    x_lo = lo.astype(BF16)
    x_hi = hi.astype(BF16)
    g = (jnp.dot(x_lo, wg_ref[0:half, :], preferred_element_type=F32)
         + jnp.dot(x_hi, wg_ref[half:2 * half, :], preferred_element_type=F32))
    u = (jnp.dot(x_lo, wu_ref[0:half, :], preferred_element_type=F32)
         + jnp.dot(x_hi, wu_ref[half:2 * half, :], preferred_element_type=F32))
    act = (_silu(g) * u).astype(BF16)
    o_ref[...] = jnp.dot(act, wd_ref[...], preferred_element_type=F32).astype(o_ref.dtype)


def _shared_expert(fpk, wg_bf, wu_bf, wd_bf):
    t, half = fpk.shape
    d = 2 * half
    ds = wg_bf.shape[1]
    tm = min(512, t)
    return pl.pallas_call(
        _shared_kernel,
        grid=(t // tm,),
        in_specs=[pl.BlockSpec((tm, half), lambda i: (i, 0)),
                  pl.BlockSpec((d, ds), lambda i: (0, 0)),
                  pl.BlockSpec((d, ds), lambda i: (0, 0)),
                  pl.BlockSpec((ds, d), lambda i: (0, 0))],
        out_specs=pl.BlockSpec((tm, d), lambda i: (i, 0)),
        out_shape=jax.ShapeDtypeStruct((t, d), BF16),
        compiler_params=_params("arbitrary"),
        name="shared_expert",
    )(fpk, wg_bf, wu_bf, wd_bf)


def _combine_kernel(dest_hbm, ys_hbm, wts_ref, sh_ref, h_ref, mod_ref, g_ref, o_ref,
                    dest_s0, dest_s1, gbuf0, gbuf1, sem_d, sem_g, *, tm, top_k, n_tiles):
    i = pl.program_id(0)
    n_rows = top_k * tm
    half = gbuf0.shape[1]
    dest_s, gbuf = (dest_s0, dest_s1), (gbuf0, gbuf1)
    last = n_tiles - 1

    def dest_copy(tile, s):
        return pltpu.make_async_copy(dest_hbm.at[tile], dest_s[s], sem_d.at[s])

    def gather_row(s, r, priority):
        row = dest_s[s][r]
        pltpu.async_copy(ys_hbm.at[pl.ds(row, 1), :], gbuf[s].at[pl.ds(r, 1), :], sem_g.at[s],
                         priority=priority)

    def gather_start(s, unrolled):
        if unrolled:
            for r in range(n_rows):
                gather_row(s, r, r % 2)
        else:
            def body(r, c):
                gather_row(s, r, 0)
                return c
            lax.fori_loop(0, n_rows, body, 0, unroll=8)

    def gather_wait(s):
        pltpu.make_async_copy(ys_hbm.at[pl.ds(0, n_rows), :], gbuf[s], sem_g.at[s]).wait()

    @pl.when(i == 0)
    def _():
        first_dest = dest_copy(0, 0)
        first_dest.start()
        first_dest.wait()
        gather_start(0, unrolled=False)
        dest_copy(jnp.minimum(1, last), 1).start()

    def run_tile(cur):
        nxt = 1 - cur
        gather_wait(cur)
        dest_copy(jnp.minimum(i + 1, last), nxt).wait()
        gather_start(nxt, unrolled=True)
        dest_copy(jnp.minimum(i + 2, last), cur).start()
        acc_lo = jnp.zeros((tm, half), F32)
        acc_hi = jnp.zeros((tm, half), F32)
        for k in range(top_k):
            lo, hi = _unpack_bf16_pair(gbuf[cur][k * tm:(k + 1) * tm, :])
            w = wts_ref[:, k:k + 1]
            acc_lo = acc_lo + w * lo
            acc_hi = acc_hi + w * hi
        moe_lo = acc_lo + sh_ref[:, 0:half].astype(F32)
        moe_hi = acc_hi + sh_ref[:, half:2 * half].astype(F32)
        ssq = (jnp.sum(moe_lo * moe_lo, axis=1, keepdims=True)
               + jnp.sum(moe_hi * moe_hi, axis=1, keepdims=True))
        r = lax.rsqrt(ssq * (1.0 / (2 * half)) + NORM_EPS)
        gate2 = mod_ref[0, 5:6, :]
        g = g_ref[...]
        o_ref[:, 0:half] = h_ref[:, 0:half] + gate2[:, 0:half] * (moe_lo * r * g[:, 0:half])
        o_ref[:, half:2 * half] = (h_ref[:, half:2 * half]
                                   + gate2[:, half:2 * half] * (moe_hi * r * g[:, half:2 * half]))

        @pl.when(i == last)
        def _():
            gather_wait(nxt)
            dest_copy(last, cur).wait()

    @pl.when(i % 2 == 0)
    def _():
        run_tile(0)

    @pl.when(i % 2 == 1)
    def _():
        run_tile(1)


def _combine(ys, dest_tiles, wts, shared, h, mod, g_post, seq, top_k):
    t, d = h.shape
    tm = COMBINE_TOKENS
    n_tiles = t // tm
    tiles_per_seq = seq // tm
    row = pl.BlockSpec((tm, d), lambda i: (i, 0))
    any_spec = pl.BlockSpec(memory_space=pl.ANY)
    return pl.pallas_call(
        functools.partial(_combine_kernel, tm=tm, top_k=top_k, n_tiles=n_tiles),
        grid=(n_tiles,),
        in_specs=[any_spec, any_spec,
                  pl.BlockSpec((tm, top_k), lambda i: (i, 0)),
                  row, row,
                  pl.BlockSpec((1, N_MOD, d), lambda i: (i // tiles_per_seq, 0, 0)),
                  pl.BlockSpec((1, d), lambda i: (0, 0))],
        out_specs=row,
        out_shape=jax.ShapeDtypeStruct((t, d), F32),
        scratch_shapes=[pltpu.SMEM((top_k * tm,), I32),
                        pltpu.SMEM((top_k * tm,), I32),
                        pltpu.VMEM((top_k * tm, d // 2), U32),
                        pltpu.VMEM((top_k * tm, d // 2), U32),
                        pltpu.SemaphoreType.DMA((2,)),
                        pltpu.SemaphoreType.DMA((2,))],
        compiler_params=_params("arbitrary"),
        name="combine",
    )(dest_tiles, ys, wts, shared, h, mod, g_post.reshape(1, d))


def _rope_tables(seq):
    half = HEAD_DIM // 2
    inv_freq = jnp.exp(-math.log(ROPE_THETA) * jnp.arange(half, dtype=F32) / half)
    ang = jnp.arange(seq, dtype=F32)[:, None] * inv_freq[None, :]
    cos, sin = jnp.cos(ang), jnp.sin(ang)
    return jnp.concatenate([cos, cos], axis=1), jnp.concatenate([-sin, sin], axis=1)


def kernel(x, c, w_mod, b_mod, g_pre_mix, g_post_mix, g_pre_ffn, g_post_ffn, w_in, w_pool, pool_scale, g_attn_out, g_pool_out, w_out, w_router, router_bias, w_gate, w_up, w_down, w_shared_gate, w_shared_up, w_shared_down):
    batch, seq, d = x.shape
    depth = w_mod.shape[0]
    n_groups, gd = w_pool.shape[1], w_pool.shape[2]
    pool_width = n_groups * gd
    attn_width = (w_in.shape[2] - pool_width) // 3
    n_heads = attn_width // HEAD_DIM
    assert n_groups == len(POOL_WINDOWS) and seq % (ATTN_BLOCK * DILATIONS[-1]) == 0
    cosf, sinf = _rope_tables(seq)

    h = x.reshape(batch * seq, d)
    for layer in range(depth):
        mod = _modulation(c, w_mod[layer], b_mod[layer])
        proj = _in_projection(h, g_pre_mix[layer], mod, w_in[layer].astype(BF16), cosf, sinf,
                              seq, attn_width)
        attn = _dilated_attention(proj, batch, seq, n_heads)
        mixed = _mixer_norms(attn, proj, w_pool[layer].astype(BF16), pool_scale[layer],
                             g_attn_out[layer], g_pool_out[layer], batch, seq)
        y = _matmul(mixed, w_out[layer].astype(BF16), F32)
        h, fpk, idx_t, wts_t, rank_t, cum = _post_mix_route(
            y, h, mod, g_post_mix[layer], g_pre_ffn[layer], w_router[layer], router_bias[layer], seq)
        block_tables, dest_tiles = _dispatch_tables(idx_t, rank_t, cum, MOE_ROWS, COMBINE_TOKENS)
        xs = _dispatch_rows(fpk, cum, block_tables)
        ys = _routed_experts(xs, block_tables, w_gate[layer], w_up[layer], w_down[layer])
        shared = _shared_expert(fpk, w_shared_gate[layer].astype(BF16),
                                w_shared_up[layer].astype(BF16), w_shared_down[layer].astype(BF16))
        h = _combine(ys, dest_tiles, wts_t.T, shared, h, mod, g_post_ffn[layer], seq, TOP_K)
    return h.reshape(batch, seq, d)
```

```python
import functools
import math

import jax
import jax.numpy as jnp
from jax import lax
from jax.experimental import pallas as pl
from jax.experimental.pallas import tpu as pltpu

F32 = jnp.float32
BF16 = jnp.bfloat16
I32 = jnp.int32
U32 = jnp.uint32

HEAD_DIM = 128
ATTN_BLOCK = 128
DILATIONS = (1, 4, 16)
POOL_WINDOWS = (2, 4, 8, 16)
POOL_HALO = 16
ROPE_THETA = 10000.0
TOP_K = 8
N_EXPERT_GROUPS = 8
TOPK_GROUPS = 4
ROUTED_SCALE = 2.5
N_MOD = 6
NORM_EPS = 1e-6
MASK_VALUE = -1e30

LANES = 128
SUBLANES = 8
VMEM_LIMIT_BYTES = 60 * 1024 * 1024

MOE_ROWS = 256
EXPERT_LOOKAHEAD = 2
COMBINE_TOKENS = 128

_NT = (((1,), (1,)), ((), ()))


def _params(*sem):
    return pltpu.CompilerParams(dimension_semantics=sem, vmem_limit_bytes=VMEM_LIMIT_BYTES)


def _rms(x, g):
    return x * lax.rsqrt(jnp.mean(x * x, axis=-1, keepdims=True) + NORM_EPS) * g


def _silu(x):
    return x * jax.nn.sigmoid(x)


def _pack_bf16_pair(x):
    n = x.shape[1] // 2
    lo = lax.bitcast_convert_type(x[:, :n].astype(BF16).astype(F32), U32)
    hi = lax.bitcast_convert_type(x[:, n:].astype(BF16).astype(F32), U32)
    return (lo >> 16) | hi


def _unpack_bf16_pair(w):
    lo = lax.bitcast_convert_type(w << 16, F32)
    hi = lax.bitcast_convert_type(w & jnp.uint32(0xFFFF0000), F32)
    return lo, hi


def _mod_kernel(c_ref, w_ref, b_ref, o_ref):
    cond = _silu(c_ref[...])
    o_ref[...] = jnp.dot(cond.astype(BF16), w_ref[...].astype(BF16),
                         preferred_element_type=F32) + b_ref[...]


def _modulation(c, w_mod, b_mod):
    b, d = c.shape
    n = w_mod.shape[1]
    bp = -(-b // SUBLANES) * SUBLANES
    tn = 512
    out = pl.pallas_call(
        _mod_kernel,
        grid=(n // tn,),
        in_specs=[pl.BlockSpec((bp, d), lambda j: (0, 0)),
                  pl.BlockSpec((d, tn), lambda j: (0, j)),
                  pl.BlockSpec((1, tn), lambda j: (0, j))],
        out_specs=pl.BlockSpec((bp, tn), lambda j: (0, j)),
        out_shape=jax.ShapeDtypeStruct((bp, n), F32),
        compiler_params=_params("arbitrary"),
        name="modulation",
    )(jnp.pad(c, ((0, bp - b), (0, 0))), w_mod, b_mod.reshape(1, n))
    return out[:b].reshape(b, N_MOD, d)


def _inproj_kernel(x_ref, g_ref, mod_ref, w_ref, cos_ref, sin_ref, o_ref, a_scr, *,
                   n_q_tiles, n_rope_tiles):
    j = pl.program_id(1)

    @pl.when(j == 0)
    def _():
        shift1 = mod_ref[0, 0:1, :]
        scale1 = mod_ref[0, 1:2, :]
        a = _rms(x_ref[...], g_ref[...]) * (1.0 + scale1) + shift1
        a_scr[...] = a.astype(BF16)

    acc = jnp.dot(a_scr[...], w_ref[...], preferred_element_type=F32)

    @pl.when(j < n_rope_tiles)
    def _():
        qk_scale = jnp.where(j < n_q_tiles, HEAD_DIM ** -0.5, 1.0)
        cosf = cos_ref[...] * qk_scale
        sinf = sin_ref[...] * qk_scale
        for hh in range(acc.shape[1] // HEAD_DIM):
            blk = acc[:, hh * HEAD_DIM:(hh + 1) * HEAD_DIM]
            rot = blk * cosf + pltpu.roll(blk, HEAD_DIM // 2, 1) * sinf
            o_ref[:, hh * HEAD_DIM:(hh + 1) * HEAD_DIM] = rot.astype(o_ref.dtype)

    @pl.when(j >= n_rope_tiles)
    def _():
        o_ref[...] = acc.astype(o_ref.dtype)


def _in_projection(x2, g, mod, w_in_bf, cosf, sinf, seq, attn_width):
    t, d = x2.shape
    n = w_in_bf.shape[1]
    tm = min(512, seq)
    tn = min(1024, attn_width)
    tiles_per_seq = seq // tm
    return pl.pallas_call(
        functools.partial(_inproj_kernel, n_q_tiles=attn_width // tn, n_rope_tiles=2 * attn_width // tn),
        grid=(t // tm, n // tn),
        in_specs=[pl.BlockSpec((tm, d), lambda i, j: (i, 0)),
                  pl.BlockSpec((1, d), lambda i, j: (0, 0)),
                  pl.BlockSpec((1, N_MOD, d), lambda i, j: (i // tiles_per_seq, 0, 0)),
                  pl.BlockSpec((d, tn), lambda i, j: (0, j)),
                  pl.BlockSpec((tm, HEAD_DIM), lambda i, j: (i % tiles_per_seq, 0)),
                  pl.BlockSpec((tm, HEAD_DIM), lambda i, j: (i % tiles_per_seq, 0))],
        out_specs=pl.BlockSpec((tm, tn), lambda i, j: (i, j)),
        out_shape=jax.ShapeDtypeStruct((t, n), BF16),
        scratch_shapes=[pltpu.VMEM((tm, d), BF16)],
        compiler_params=_params("arbitrary", "arbitrary"),
        name="in_projection",
    )(x2, g.reshape(1, d), mod, w_in_bf, cosf, sinf)


def _attn_kernel(q_ref, k_ref, v_ref, o_ref, qf, kf, vf, qs, ks, vs, accs, maxs, dens, stage, *, seq):
    blk = ATTN_BLOCK
    n_blk = seq // blk
    qf[...] = q_ref[...].astype(F32)
    kf[...] = k_ref[...].astype(F32)
    vf[...] = v_ref[...].astype(F32)
    vs[:, HEAD_DIM:2 * HEAD_DIM] = jnp.ones((vs.shape[0], HEAD_DIM), BF16)

    qi = lax.broadcasted_iota(I32, (blk, blk), 0)
    ki = lax.broadcasted_iota(I32, (blk, blk), 1)
    causal_bias = jnp.where(ki <= qi, 0.0, MASK_VALUE)
    upper_bias = jnp.where(ki >= qi, 0.0, MASK_VALUE)

    for bi, dil in enumerate(DILATIONS):
        sub_len = seq // dil
        nbk = sub_len // blk
        lead = blk if nbk > 1 else 0
        pitch = sub_len + lead
        def to_sub_order(res, carry, dil=dil, sub_len=sub_len, lead=lead, pitch=pitch):
            base = pl.multiple_of(jnp.asarray(res * pitch, I32), blk)
            q0 = pl.multiple_of(jnp.asarray(res * sub_len, I32), blk)
            if dil == 1:
                q_src, k_src, v_src = q_ref[...], k_ref[...], v_ref[...]
            else:
                q_src = qf[pl.ds(res, sub_len, stride=dil), :].astype(BF16)
                k_src = kf[pl.ds(res, sub_len, stride=dil), :].astype(BF16)
                v_src = vf[pl.ds(res, sub_len, stride=dil), :].astype(BF16)
            qs[pl.ds(q0, sub_len), :] = q_src
            if lead:
                ks[pl.ds(base, lead), :] = jnp.zeros((lead, HEAD_DIM), BF16)
                vs[pl.ds(base, lead), 0:HEAD_DIM] = jnp.zeros((lead, HEAD_DIM), BF16)
            ks[pl.ds(base + lead, sub_len), :] = k_src
            vs[pl.ds(base + lead, sub_len), 0:HEAD_DIM] = v_src
            return carry

        if dil == 1:
            to_sub_order(0, 0)
        else:
            lax.fori_loop(0, dil, to_sub_order, 0)

        def blocks(ref, offset, dil=dil, pitch=pitch, sub_len=sub_len, nbk=nbk):
            parts = [ref[res * pitch + offset:res * pitch + offset + sub_len, :]
                     .reshape(nbk, blk, ref.shape[1]) for res in range(dil)]
            return parts[0] if dil == 1 else jnp.concatenate(parts, axis=0)

        q3 = qs[...].reshape(n_blk, blk, HEAD_DIM)
        s_own = jnp.einsum('nqc,nkc->nqk', q3, blocks(ks, lead),
                           preferred_element_type=F32) + causal_bias[None]
        if nbk > 1:
            first = (lax.broadcasted_iota(I32, (n_blk, blk, blk), 0) & (nbk - 1)) == 0
            prev_bias = jnp.where(first, MASK_VALUE, upper_bias[None])
            s_prev = jnp.einsum('nqc,nkc->nqk', q3, blocks(ks, 0),
                                preferred_element_type=F32) + prev_bias
            m = jnp.max(jnp.maximum(s_own, s_prev), axis=-1, keepdims=True)
            accx = (jnp.einsum('nqk,nkc->nqc', jnp.exp(s_own - m).astype(BF16), blocks(vs, lead),
                               preferred_element_type=F32)
                    + jnp.einsum('nqk,nkc->nqc', jnp.exp(s_prev - m).astype(BF16), blocks(vs, 0),
                                 preferred_element_type=F32))
        else:
            m = jnp.max(s_own, axis=-1, keepdims=True)
            accx = jnp.einsum('nqk,nkc->nqc', jnp.exp(s_own - m).astype(BF16), blocks(vs, lead),
                              preferred_element_type=F32)
        m_b = jnp.broadcast_to(m, (n_blk, blk, HEAD_DIM))
        results = (accx[:, :, 0:HEAD_DIM], accx[:, :, HEAD_DIM:2 * HEAD_DIM], m_b)
        if dil == 1:
            for dst, val in zip((accs, dens, maxs), results):
                dst[bi] = val.reshape(seq, HEAD_DIM)
        else:
            for j, val in enumerate(results):
                stage[j] = val.reshape(seq, HEAD_DIM)

            def to_natural_order(res, carry, dil=dil, sub_len=sub_len, bi=bi):
                r0 = pl.multiple_of(jnp.asarray(res * sub_len, I32), blk)
                for j, dst in enumerate((accs, dens, maxs)):
                    dst[bi, pl.ds(res, sub_len, stride=dil), :] = stage[j, pl.ds(r0, sub_len), :]
                return carry

            lax.fori_loop(0, dil, to_natural_order, 0)

    def merge(c, carry):
        r0 = pl.multiple_of(c * blk, blk)
        ms = [maxs[bi, pl.ds(r0, blk), :] for bi in range(len(DILATIONS))]
        m_all = functools.reduce(jnp.maximum, ms)
        ws = [jnp.exp(m - m_all) for m in ms]
        den = sum(w * dens[bi, pl.ds(r0, blk), :] for bi, w in enumerate(ws))
        num = sum(w * accs[bi, pl.ds(r0, blk), :] for bi, w in enumerate(ws))
        o_ref[pl.ds(r0, blk), :] = (num / den).astype(o_ref.dtype)
        return carry

    lax.fori_loop(0, n_blk, merge, 0, unroll=2)


def _dilated_attention(proj, batch, seq, n_heads):
    t = proj.shape[0]
    nb = len(DILATIONS)
    key_rows = max(d * (seq // d + (ATTN_BLOCK if seq // d > ATTN_BLOCK else 0)) for d in DILATIONS)
    return pl.pallas_call(
        functools.partial(_attn_kernel, seq=seq),
        grid=(batch, n_heads),
        in_specs=[pl.BlockSpec((seq, HEAD_DIM), lambda b, h: (b, h)),
                  pl.BlockSpec((seq, HEAD_DIM), lambda b, h: (b, n_heads + h)),
                  pl.BlockSpec((seq, HEAD_DIM), lambda b, h: (b, 2 * n_heads + h))],
        out_specs=pl.BlockSpec((seq, HEAD_DIM), lambda b, h: (b, h)),
        out_shape=jax.ShapeDtypeStruct((t, n_heads * HEAD_DIM), BF16),
        scratch_shapes=[pltpu.VMEM((seq, HEAD_DIM), F32),
                        pltpu.VMEM((seq, HEAD_DIM), F32),
                        pltpu.VMEM((seq, HEAD_DIM), F32),
                        pltpu.VMEM((seq, HEAD_DIM), BF16),
                        pltpu.VMEM((key_rows, HEAD_DIM), BF16),
                        pltpu.VMEM((key_rows, 2 * HEAD_DIM), BF16),
                        pltpu.VMEM((nb, seq, HEAD_DIM), F32),
                        pltpu.VMEM((nb, seq, HEAD_DIM), F32),
                        pltpu.VMEM((nb, seq, HEAD_DIM), F32),
                        pltpu.VMEM((3, seq, HEAD_DIM), F32)],
        compiler_params=_params("arbitrary", "arbitrary"),
        name="dilated_attention",
    )(proj, proj, proj)


def _mix_kernel(attn_ref, u_ref, wp_ref, ps_ref, ga_ref, gp_ref, o_ref, ext, *, tr, aw, gd):
    j = pl.program_id(1)
    n_groups = len(POOL_WINDOWS)

    @pl.when(j == 0)
    def _():
        ext[0:POOL_HALO, :] = jnp.zeros((POOL_HALO, ext.shape[1]), BF16)

    @pl.when(j > 0)
    def _():
        ext[0:POOL_HALO, :] = ext[tr:tr + POOL_HALO, :]

    ext[POOL_HALO:POOL_HALO + tr, :] = u_ref[...]

    o_ref[:, 0:aw] = _rms(attn_ref[...].astype(F32), ga_ref[...]).astype(o_ref.dtype)

    pos = j * tr + lax.broadcasted_iota(I32, (tr, 1), 0)
    diff = (lax.broadcasted_iota(I32, (tr, tr + POOL_HALO), 0) + POOL_HALO
            - lax.broadcasted_iota(I32, (tr, tr + POOL_HALO), 1))
    ys = []
    ssq = jnp.zeros((tr, 1), F32)
    for g, w in enumerate(POOL_WINDOWS):
        band = ((diff >= 0) & (diff < w)).astype(BF16)
        win = jnp.dot(band, ext[:, g * gd:(g + 1) * gd], preferred_element_type=F32)
        cnt = jnp.minimum(pos + 1, w).astype(F32)
        mix = win * (1.0 / cnt) - u_ref[:, g * gd:(g + 1) * gd].astype(F32)
        y = jnp.dot(mix.astype(BF16), wp_ref[g], preferred_element_type=F32) * ps_ref[g:g + 1, :]
        ys.append(y)
        ssq = ssq + jnp.sum(y * y, axis=1, keepdims=True)
    r = lax.rsqrt(ssq * (1.0 / (n_groups * gd)) + NORM_EPS)
    for g in range(n_groups):
        o_ref[:, aw + g * gd:aw + (g + 1) * gd] = (
            ys[g] * r * gp_ref[:, g * gd:(g + 1) * gd]).astype(o_ref.dtype)


def _mixer_norms(attn, proj, w_pool_bf, pool_scale, g_attn, g_pool, batch, seq):
    t, aw = attn.shape
    n_groups, gd, _ = w_pool_bf.shape
    pw = n_groups * gd
    tr = min(256, seq)
    tiles = seq // tr
    u_col_block = (proj.shape[1] - pw) // pw
    return pl.pallas_call(
        functools.partial(_mix_kernel, tr=tr, aw=aw, gd=gd),
        grid=(batch, tiles),
        in_specs=[pl.BlockSpec((tr, aw), lambda b, j: (b * tiles + j, 0)),
                  pl.BlockSpec((tr, pw), lambda b, j: (b * tiles + j, u_col_block)),
                  pl.BlockSpec((n_groups, gd, gd), lambda b, j: (0, 0, 0)),
                  pl.BlockSpec((n_groups, gd), lambda b, j: (0, 0)),
                  pl.BlockSpec((1, aw), lambda b, j: (0, 0)),
                  pl.BlockSpec((1, pw), lambda b, j: (0, 0))],
        out_specs=pl.BlockSpec((tr, aw + pw), lambda b, j: (b * tiles + j, 0)),
        out_shape=jax.ShapeDtypeStruct((t, aw + pw), BF16),
        scratch_shapes=[pltpu.VMEM((tr + POOL_HALO, pw), BF16)],
        compiler_params=_params("arbitrary", "arbitrary"),
        name="mixer_norms",
    )(attn, proj, w_pool_bf, pool_scale, g_attn.reshape(1, aw), g_pool.reshape(1, pw))


def _matmul_kernel(a_ref, b_ref, o_ref):
    o_ref[...] = jnp.dot(a_ref[...], b_ref[...], preferred_element_type=F32).astype(o_ref.dtype)


def _matmul(a, b, out_dtype, tm=1024, tn=1024):
    m, k = a.shape
    n = b.shape[1]
    tm, tn = min(tm, m), min(tn, n)
    return pl.pallas_call(
        _matmul_kernel,
        grid=(m // tm, n // tn),
        in_specs=[pl.BlockSpec((tm, k), lambda i, j: (i, 0)),
                  pl.BlockSpec((k, tn), lambda i, j: (0, j))],
        out_specs=pl.BlockSpec((tm, tn), lambda i, j: (i, j)),
        out_shape=jax.ShapeDtypeStruct((m, n), out_dtype),
        compiler_params=_params("arbitrary", "arbitrary"),
        name="out_projection",
    )(a, b)


def _split_bf16(x):
    hi = x.astype(BF16)
    lo = (x - hi.astype(F32)).astype(BF16)
    return hi, lo


def _route(f, w_t, bias):
    f_hi, f_lo = _split_bf16(f)
    w_hi, w_lo = _split_bf16(w_t)
    logits = (lax.dot_general(w_hi, f_hi, _NT, preferred_element_type=F32)
              + lax.dot_general(w_hi, f_lo, _NT, preferred_element_type=F32)
              + lax.dot_general(w_lo, f_hi, _NT, preferred_element_type=F32))
    n_exp, tm = logits.shape
    per = n_exp // N_EXPERT_GROUPS
    scores = jax.nn.sigmoid(logits)
    biased = scores + bias
    ninf = -jnp.inf

    b3 = biased.reshape(N_EXPERT_GROUPS, per, tm)
    memb = lax.broadcasted_iota(I32, b3.shape, 1)
    m1 = jnp.max(b3, axis=1, keepdims=True)
    i1 = jnp.min(jnp.where(b3 == m1, memb, per), axis=1, keepdims=True)
    m2 = jnp.max(jnp.where(memb == i1, ninf, b3), axis=1, keepdims=True)
    gscore = (m1 + m2).reshape(N_EXPERT_GROUPS, tm)

    gid = lax.broadcasted_iota(I32, gscore.shape, 0)
    gmask = jnp.zeros(gscore.shape, jnp.bool_)
    cur = gscore
    for _ in range(TOPK_GROUPS):
        mx = jnp.max(cur, axis=0, keepdims=True)
        ix = jnp.min(jnp.where(cur == mx, gid, N_EXPERT_GROUPS), axis=0, keepdims=True)
        sel = gid == ix
        gmask = gmask | sel
        cur = jnp.where(sel, ninf, cur)
    emask = jnp.broadcast_to(gmask.reshape(N_EXPERT_GROUPS, 1, tm), b3.shape).reshape(n_exp, tm)

    masked = jnp.where(emask, biased, ninf)
    eid = lax.broadcasted_iota(I32, masked.shape, 0)
    idx_rows, w_rows, sels = [], [], []
    chosen = jnp.zeros(masked.shape, F32)
    for _ in range(TOP_K):
        mx = jnp.max(masked, axis=0, keepdims=True)
        ix = jnp.min(jnp.where(masked == mx, eid, n_exp), axis=0, keepdims=True)
        sel = eid == ix
        idx_rows.append(ix)
        w_rows.append(jnp.sum(jnp.where(sel, scores, 0.0), axis=0, keepdims=True))
        sels.append(sel)
        chosen = jnp.where(sel, 1.0, chosen)
        masked = jnp.where(sel, ninf, masked)
    wsel = jnp.concatenate(w_rows, axis=0)
    wts = wsel / jnp.sum(wsel, axis=0, keepdims=True) * ROUTED_SCALE
    return jnp.concatenate(idx_rows, axis=0), wts, sels, chosen


def _postmix_kernel(y_ref, x_ref, mod_ref, gpost_ref, gpre_ref, wt_ref, bias_ref,
                    h_ref, fpk_ref, idx_ref, wts_ref, rank_ref, cum_ref, carry):
    i = pl.program_id(0)

    @pl.when(i == 0)
    def _():
        carry[...] = jnp.zeros(carry.shape, F32)

    gate1 = mod_ref[0, 2:3, :]
    shift2 = mod_ref[0, 3:4, :]
    scale2 = mod_ref[0, 4:5, :]
    h = x_ref[...] + gate1 * _rms(y_ref[...], gpost_ref[...])
    h_ref[...] = h
    f = _rms(h, gpre_ref[...]) * (1.0 + scale2) + shift2
    fpk_ref[...] = _pack_bf16_pair(f)

    idx, wts, sels, chosen = _route(f, wt_ref[...], bias_ref[...])
    tm = f.shape[0]
    upper = (lax.broadcasted_iota(I32, (tm, tm), 0)
             <= lax.broadcasted_iota(I32, (tm, tm), 1)).astype(BF16)
    cum = jnp.dot(chosen.astype(BF16), upper, preferred_element_type=F32) + carry[...]
    carry[...] = cum[:, tm - 1:tm]
    idx_ref[...] = idx
    wts_ref[...] = wts
    cum_ref[...] = cum
    rank_ref[...] = jnp.concatenate(
        [jnp.sum(jnp.where(sel, cum, 0.0), axis=0, keepdims=True) for sel in sels],
        axis=0).astype(I32) - 1


def _post_mix_route(y, x2, mod, g_post, g_pre, w_router, router_bias, seq):
    t, d = x2.shape
    n_exp = w_router.shape[1]
    tm = min(256, seq)
    tiles_per_seq = seq // tm
    row = pl.BlockSpec((tm, d), lambda i: (i, 0))
    vec = pl.BlockSpec((1, d), lambda i: (0, 0))
    kt = pl.BlockSpec((TOP_K, tm), lambda i: (0, i))
    return pl.pallas_call(
        _postmix_kernel,
        grid=(t // tm,),
        in_specs=[row, row,
                  pl.BlockSpec((1, N_MOD, d), lambda i: (i // tiles_per_seq, 0, 0)),
                  vec, vec,
                  pl.BlockSpec((n_exp, d), lambda i: (0, 0)),
                  pl.BlockSpec((n_exp, 1), lambda i: (0, 0))],
        out_specs=[row, pl.BlockSpec((tm, d // 2), lambda i: (i, 0)), kt, kt, kt,
                   pl.BlockSpec((n_exp, tm), lambda i: (0, i))],
        out_shape=[jax.ShapeDtypeStruct((t, d), F32),
                   jax.ShapeDtypeStruct((t, d // 2), U32),
                   jax.ShapeDtypeStruct((TOP_K, t), I32),
                   jax.ShapeDtypeStruct((TOP_K, t), F32),
                   jax.ShapeDtypeStruct((TOP_K, t), I32),
                   jax.ShapeDtypeStruct((n_exp, t), F32)],
        scratch_shapes=[pltpu.VMEM((n_exp, 1), F32)],
        compiler_params=_params("arbitrary"),
        name="post_mix_route",
    )(y, x2, mod, g_post.reshape(1, d), g_pre.reshape(1, d), w_router.T, router_bias.reshape(n_exp, 1))


def _dispatch_tables(idx_t, rank_t, cum, rows_per_block, tokens_per_tile):
    k, t = idx_t.shape
    n_exp = cum.shape[0]
    n = k * t
    n_blocks = n // rows_per_block + n_exp
    experts = jnp.arange(n_exp, dtype=I32)
    counts = cum[:, -1].astype(I32)
    padded = (counts + rows_per_block - 1) // rows_per_block * rows_per_block
    pad_ends = jnp.cumsum(padded)
    pad_starts = pad_ends - padded
    n_used = (pad_ends[-1] // rows_per_block).astype(I32)
    n_tab = n_blocks + EXPERT_LOOKAHEAD
    blk_start = jnp.arange(n_tab, dtype=I32) * rows_per_block
    block_e = jnp.minimum(jnp.sum((pad_ends[None, :] <= blk_start[:, None]).astype(I32), axis=1),
                          n_exp - 1).astype(I32)
    of_block = block_e[:, None] == experts[None, :]
    first_row = blk_start - jnp.sum(jnp.where(of_block, pad_starts[None, :], 0), axis=1)
    prev_e = jnp.concatenate([jnp.full((1,), -1, I32), block_e[:-1]])
    first = (block_e != prev_e).astype(I32)
    after = jnp.sum(jnp.where(of_block, pad_ends[None, :], 0), axis=1) // rows_per_block
    next_e = jnp.where(after < n_used, block_e[jnp.minimum(after, n_tab - 1)], -1).astype(I32)
    dest = jnp.sum(jnp.where(idx_t[:, :, None] == experts[None, None, :],
                             pad_starts[None, None, :], 0), axis=2) + rank_t
    n_tiles = t // tokens_per_tile
    dest_tiles = dest.reshape(k, n_tiles, tokens_per_tile).transpose(1, 0, 2).reshape(
        n_tiles, k * tokens_per_tile).astype(I32)
    return (block_e, first, next_e, first_row.astype(I32), n_used.reshape(1)), dest_tiles


def _dispatch_kernel(be_ref, row0_ref, nused_ref,
                     cum_hbm, ends_hbm, fpk_hbm,
                     xs_hbm,
                     cum_v, ends_v, tok_v0, tok_v1, tok_s0, tok_s1, xs0, xs1, zero_v,
                     sem_cum, sem_tok, sem_x, sem_o, *, rows, n_tokens):
    i = pl.program_id(0)
    n_used = nused_ref[0]
    n_buckets = cum_v.shape[1]
    tok_v, tok_s, xs = (tok_v0, tok_v1), (tok_s0, tok_s1), (xs0, xs1)

    def out_copy(src, blk, slot):
        return pltpu.make_async_copy(src, xs_hbm.at[pl.ds(pl.multiple_of(blk * rows, rows), rows), :],
                                     sem_o.at[slot])

    def tok_copy(slot):
        return pltpu.make_async_copy(tok_v[slot], tok_s[slot], sem_tok.at[slot])

    def invert(blk, slot):
        e = be_ref[blk]
        row0 = row0_ref[blk]
        ends = ends_v[pl.ds(e, 1), :]
        counts = cum_v[e]
        c_hi = jnp.floor(counts * (1.0 / LANES))
        c_lo = counts - c_hi * LANES
        if n_buckets < LANES:
            fill = jnp.zeros((LANES - n_buckets, LANES), F32)
            c_hi = jnp.concatenate([c_hi, fill], axis=0)
            c_lo = jnp.concatenate([c_lo, fill], axis=0)
        c_hi = c_hi.astype(BF16)
        c_lo = c_lo.astype(BF16)
        ones = jnp.ones((SUBLANES, LANES), BF16)
        lane = lax.broadcasted_iota(I32, (LANES, LANES), 1)
        for c in range(rows // LANES):
            j = (row0 + c * LANES + lax.broadcasted_iota(I32, (LANES, 1), 0)).astype(F32)
            done = jnp.where(ends <= j, 1.0, 0.0)
            before = jnp.where(lane == 0, 1.0, pltpu.roll(done, 1, 1))
            pick = ((1.0 - done) * before).astype(BF16)
            in_bucket = (jnp.dot(pick, c_hi, preferred_element_type=F32) * LANES
                         + jnp.dot(pick, c_lo, preferred_element_type=F32))
            below = jnp.where(in_bucket <= j, 1.0, 0.0).astype(BF16)
            n_done = lax.dot_general(ones, done.astype(BF16), _NT, preferred_element_type=F32)
            n_below = lax.dot_general(ones, below, _NT, preferred_element_type=F32)
            tok = jnp.minimum(n_done * LANES + n_below, n_tokens - 1.0)
            tok_v[slot][:, c * LANES:(c + 1) * LANES] = tok.astype(I32)

    def gather_row(slot, r, priority):
        tok = tok_s[slot][0, r]
        pltpu.async_copy(fpk_hbm.at[pl.ds(tok, 1), :], xs[slot].at[pl.ds(r, 1), :],
                         sem_x.at[slot], priority=priority)

    def gather_start(slot, unrolled):
        if unrolled:
            for r in range(rows):
                gather_row(slot, r, r % 2)
        else:
            def body(r, c):
                gather_row(slot, r, 0)
                return c
            lax.fori_loop(0, rows, body, 0, unroll=8)

    def gather_wait(slot):
        pltpu.make_async_copy(fpk_hbm.at[pl.ds(0, rows), :], xs[slot], sem_x.at[slot]).wait()

    @pl.when(i == 0)
    def _():
        zero_v[...] = jnp.zeros(zero_v.shape, zero_v.dtype)

    @pl.when((i == 0) & (n_used > 0))
    def _():
        cum_copy = pltpu.make_async_copy(cum_hbm, cum_v, sem_cum.at[0])
        ends_copy = pltpu.make_async_copy(ends_hbm, ends_v, sem_cum.at[1])
        cum_copy.start()
        ends_copy.start()
        cum_copy.wait()
        ends_copy.wait()
        invert(0, 0)
        tok_copy(0).start()
        tok_copy(0).wait()
        gather_start(0, unrolled=False)
        invert(1, 1)
        tok_copy(1).start()

    def run_block(cur):
        nxt = 1 - cur
        gather_wait(cur)
        out_copy(xs[cur], i, cur).start()
        tok_copy(nxt).wait()

        @pl.when(i >= 1)
        def _():
            out_copy(xs[nxt], i - 1, nxt).wait()

        gather_start(nxt, unrolled=True)
        invert(i + 2, cur)
        tok_copy(cur).start()

        @pl.when(i == n_used - 1)
        def _():
            gather_wait(nxt)
            tok_copy(cur).wait()
            out_copy(xs[cur], i, cur).wait()

    @pl.when((i < n_used) & (i % 2 == 0))
    def _():
        run_block(0)

    @pl.when((i < n_used) & (i % 2 == 1))
    def _():
        run_block(1)

    @pl.when(i >= n_used)
    def _():
        tail = out_copy(zero_v, i, 0)
        tail.start()
        tail.wait()


def _dispatch_rows(fpk, cum, block_tables):
    block_e, _, _, first_row, n_used = block_tables
    t, half = fpk.shape
    n_exp = cum.shape[0]
    rows = MOE_ROWS
    n_blocks = block_e.shape[0] - EXPERT_LOOKAHEAD
    n_buckets = t // LANES
    assert n_buckets <= LANES, "two-level token search covers at most 128 buckets of 128 tokens"
    cum3 = cum.reshape(n_exp, n_buckets, LANES)
    ends = jnp.pad(cum3[:, :, LANES - 1], ((0, 0), (0, LANES - n_buckets)),
                   constant_values=float(2 * t))
    any_spec = pl.BlockSpec(memory_space=pl.ANY)
    grid_spec = pltpu.PrefetchScalarGridSpec(
        num_scalar_prefetch=3,
        grid=(n_blocks,),
        in_specs=[any_spec, any_spec, any_spec],
        out_specs=any_spec,
        scratch_shapes=[pltpu.VMEM((n_exp, n_buckets, LANES), F32),
                        pltpu.VMEM((n_exp, LANES), F32),
                        pltpu.VMEM((SUBLANES, rows), I32),
                        pltpu.VMEM((SUBLANES, rows), I32),
                        pltpu.SMEM((SUBLANES, rows), I32),
                        pltpu.SMEM((SUBLANES, rows), I32),
                        pltpu.VMEM((rows, half), U32),
                        pltpu.VMEM((rows, half), U32),
                        pltpu.VMEM((rows, half), U32),
                        pltpu.SemaphoreType.DMA((2,)),
                        pltpu.SemaphoreType.DMA((2,)),
                        pltpu.SemaphoreType.DMA((2,)),
                        pltpu.SemaphoreType.DMA((2,))])
    return pl.pallas_call(
        functools.partial(_dispatch_kernel, rows=rows, n_tokens=t),
        grid_spec=grid_spec,
        out_shape=jax.ShapeDtypeStruct((n_blocks * rows, half), U32),
        compiler_params=_params("arbitrary"),
        name="dispatch_rows",
    )(block_e, first_row, n_used, cum3, ends, fpk)


def _expert_kernel(be_ref, first_ref, nexte_ref, nused_ref,
                   x_ref, wg_hbm, wu_hbm, wd_hbm,
                   o_ref,
                   stage_g, stage_u, stage_d, wg, wu, wd, sem_w):
    i = pl.program_id(0)
    n_used = nused_ref[0]
    half = wd.shape[1] // 2

    def weight_pairs(e):
        return ((wg_hbm.at[e], stage_g), (wu_hbm.at[e], stage_u), (wd_hbm.at[e], stage_d))

    def weights_start(e):
        for k, (src, dst) in enumerate(weight_pairs(e)):
            pltpu.async_copy(src, dst, sem_w.at[k], priority=1)

    def weights_wait(e):
        for k, (src, dst) in enumerate(weight_pairs(e)):
            pltpu.make_async_copy(src, dst, sem_w.at[k]).wait()

    def cast_weights(src, dst):
        chunk = math.gcd(256, src.shape[0])
        def body(c, carry):
            r0 = pl.multiple_of(c * chunk, chunk)
            dst[pl.ds(r0, chunk), :] = src[pl.ds(r0, chunk), :].astype(BF16)
            return carry
        lax.fori_loop(0, src.shape[0] // chunk, body, 0)

    @pl.when((i == 0) & (n_used > 0))
    def _():
        weights_start(be_ref[0])

    @pl.when(i < n_used)
    def _():
        @pl.when(first_ref[i] == 1)
        def _():
            weights_wait(be_ref[i])
            cast_weights(stage_g, wg)
            cast_weights(stage_u, wu)
            cast_weights(stage_d, wd)

            @pl.when(nexte_ref[i] >= 0)
            def _():
                weights_start(nexte_ref[i])

        lo, hi = _unpack_bf16_pair(x_ref[...])
        x_lo = lo.astype(BF16)
        x_hi = hi.astype(BF16)
        g = (jnp.dot(x_lo, wg[0:half, :], preferred_element_type=F32)
             + jnp.dot(x_hi, wg[half:2 * half, :], preferred_element_type=F32))
        u = (jnp.dot(x_lo, wu[0:half, :], preferred_element_type=F32)
             + jnp.dot(x_hi, wu[half:2 * half, :], preferred_element_type=F32))
        act = (_silu(g) * u).astype(BF16)
        o_ref[...] = _pack_bf16_pair(jnp.dot(act, wd[...], preferred_element_type=F32))

    @pl.when(i >= n_used)
    def _():
        o_ref[...] = jnp.zeros(o_ref.shape, o_ref.dtype)


def _routed_experts(xs, block_tables, w_gate, w_up, w_down):
    block_e, first, next_e, _, n_used = block_tables
    n_rows, half = xs.shape
    d = 2 * half
    n_exp, _, de = w_gate.shape
    rows = MOE_ROWS
    n_blocks = n_rows // rows
    any_spec = pl.BlockSpec(memory_space=pl.ANY)
    grid_spec = pltpu.PrefetchScalarGridSpec(
        num_scalar_prefetch=4,
        grid=(n_blocks,),
        in_specs=[pl.BlockSpec((rows, half), lambda i, *_: (i, 0)), any_spec, any_spec, any_spec],
        out_specs=pl.BlockSpec((rows, half), lambda i, *_: (i, 0)),
        scratch_shapes=[pltpu.VMEM((d, de), F32),
                        pltpu.VMEM((d, de), F32),
                        pltpu.VMEM((de, d), F32),
                        pltpu.VMEM((d, de), BF16),
                        pltpu.VMEM((d, de), BF16),
                        pltpu.VMEM((de, d), BF16),
                        pltpu.SemaphoreType.DMA((3,))])
    return pl.pallas_call(
        _expert_kernel,
        grid_spec=grid_spec,
        out_shape=jax.ShapeDtypeStruct((n_rows, half), U32),
        compiler_params=_params("arbitrary"),
        name="routed_experts",
    )(block_e, first, next_e, n_used, xs, w_gate, w_up, w_down)


def _shared_kernel(fpk_ref, wg_ref, wu_ref, wd_ref, o_ref):
    half = fpk_ref.shape[1]
    lo, hi = _unpack_bf16_pair(fpk_ref[...])
    x_lo = lo.astype(BF16)
    x_hi = hi.astype(BF16)
    g = (jnp.dot(x_lo, wg_ref[0:half, :], preferred_element_type=F32)
         + jnp.dot(x_hi, wg_ref[half:2 * half, :], preferred_element_type=F32))
    u = (jnp.dot(x_lo, wu_ref[0:half, :], preferred_element_type=F32)
         + jnp.dot(x_hi, wu_ref[half:2 * half, :], preferred_element_type=F32))
    act = (_silu(g) * u).astype(BF16)
    o_ref[...] = jnp.dot(act, wd_ref[...], preferred_element_type=F32).astype(o_ref.dtype)


def _shared_expert(fpk, wg_bf, wu_bf, wd_bf):
    t, half = fpk.shape
    d = 2 * half
    ds = wg_bf.shape[1]
    tm = min(512, t)
    return pl.pallas_call(
        _shared_kernel,
        grid=(t // tm,),
        in_specs=[pl.BlockSpec((tm, half), lambda i: (i, 0)),
                  pl.BlockSpec((d, ds), lambda i: (0, 0)),
                  pl.BlockSpec((d, ds), lambda i: (0, 0)),
                  pl.BlockSpec((ds, d), lambda i: (0, 0))],
        out_specs=pl.BlockSpec((tm, d), lambda i: (i, 0)),
        out_shape=jax.ShapeDtypeStruct((t, d), BF16),
        compiler_params=_params("arbitrary"),
        name="shared_expert",
    )(fpk, wg_bf, wu_bf, wd_bf)


def _combine_kernel(dest_hbm, ys_hbm, wts_ref, sh_ref, h_ref, mod_ref, g_ref, o_ref,
                    dest_s0, dest_s1, gbuf0, gbuf1, sem_d, sem_g, *, tm, top_k, n_tiles):
    i = pl.program_id(0)
    n_rows = top_k * tm
    half = gbuf0.shape[1]
    dest_s, gbuf = (dest_s0, dest_s1), (gbuf0, gbuf1)
    last = n_tiles - 1

    def dest_copy(tile, s):
        return pltpu.make_async_copy(dest_hbm.at[tile], dest_s[s], sem_d.at[s])

    def gather_row(s, r, priority):
        row = dest_s[s][r]
        pltpu.async_copy(ys_hbm.at[pl.ds(row, 1), :], gbuf[s].at[pl.ds(r, 1), :], sem_g.at[s],
                         priority=priority)

    def gather_start(s, unrolled):
        if unrolled:
            for r in range(n_rows):
                gather_row(s, r, r % 2)
        else:
            def body(r, c):
                gather_row(s, r, 0)
                return c
            lax.fori_loop(0, n_rows, body, 0, unroll=8)

    def gather_wait(s):
        pltpu.make_async_copy(ys_hbm.at[pl.ds(0, n_rows), :], gbuf[s], sem_g.at[s]).wait()

    @pl.when(i == 0)
    def _():
        first_dest = dest_copy(0, 0)
        first_dest.start()
        first_dest.wait()
        gather_start(0, unrolled=False)
        dest_copy(jnp.minimum(1, last), 1).start()

    def run_tile(cur):
        nxt = 1 - cur
        gather_wait(cur)
        dest_copy(jnp.minimum(i + 1, last), nxt).wait()
        gather_start(nxt, unrolled=True)
        dest_copy(jnp.minimum(i + 2, last), cur).start()
        acc_lo = jnp.zeros((tm, half), F32)
        acc_hi = jnp.zeros((tm, half), F32)
        for k in range(top_k):
            lo, hi = _unpack_bf16_pair(gbuf[cur][k * tm:(k + 1) * tm, :])
            w = wts_ref[:, k:k + 1]
            acc_lo = acc_lo + w * lo
            acc_hi = acc_hi + w * hi
        moe_lo = acc_lo + sh_ref[:, 0:half].astype(F32)
        moe_hi = acc_hi + sh_ref[:, half:2 * half].astype(F32)
        ssq = (jnp.sum(moe_lo * moe_lo, axis=1, keepdims=True)
               + jnp.sum(moe_hi * moe_hi, axis=1, keepdims=True))
        r = lax.rsqrt(ssq * (1.0 / (2 * half)) + NORM_EPS)
        gate2 = mod_ref[0, 5:6, :]
        g = g_ref[...]
        o_ref[:, 0:half] = h_ref[:, 0:half] + gate2[:, 0:half] * (moe_lo * r * g[:, 0:half])
        o_ref[:, half:2 * half] = (h_ref[:, half:2 * half]
                                   + gate2[:, half:2 * half] * (moe_hi * r * g[:, half:2 * half]))

        @pl.when(i == last)
        def _():
            gather_wait(nxt)
            dest_copy(last, cur).wait()

    @pl.when(i % 2 == 0)
    def _():
        run_tile(0)

    @pl.when(i % 2 == 1)
    def _():
        run_tile(1)


def _combine(ys, dest_tiles, wts, shared, h, mod, g_post, seq, top_k):
    t, d = h.shape
    tm = COMBINE_TOKENS
    n_tiles = t // tm
    tiles_per_seq = seq // tm
    row = pl.BlockSpec((tm, d), lambda i: (i, 0))
    any_spec = pl.BlockSpec(memory_space=pl.ANY)
    return pl.pallas_call(
        functools.partial(_combine_kernel, tm=tm, top_k=top_k, n_tiles=n_tiles),
        grid=(n_tiles,),
        in_specs=[any_spec, any_spec,
                  pl.BlockSpec((tm, top_k), lambda i: (i, 0)),
                  row, row,
                  pl.BlockSpec((1, N_MOD, d), lambda i: (i // tiles_per_seq, 0, 0)),
                  pl.BlockSpec((1, d), lambda i: (0, 0))],
        out_specs=row,
        out_shape=jax.ShapeDtypeStruct((t, d), F32),
        scratch_shapes=[pltpu.SMEM((top_k * tm,), I32),
                        pltpu.SMEM((top_k * tm,), I32),
                        pltpu.VMEM((top_k * tm, d // 2), U32),
                        pltpu.VMEM((top_k * tm, d // 2), U32),
                        pltpu.SemaphoreType.DMA((2,)),
                        pltpu.SemaphoreType.DMA((2,))],
        compiler_params=_params("arbitrary"),
        name="combine",
    )(dest_tiles, ys, wts, shared, h, mod, g_post.reshape(1, d))


def _rope_tables(seq):
    half = HEAD_DIM // 2
    inv_freq = jnp.exp(-math.log(ROPE_THETA) * jnp.arange(half, dtype=F32) / half)
    ang = jnp.arange(seq, dtype=F32)[:, None] * inv_freq[None, :]
    cos, sin = jnp.cos(ang), jnp.sin(ang)
    return jnp.concatenate([cos, cos], axis=1), jnp.concatenate([-sin, sin], axis=1)


def kernel(x, c, w_mod, b_mod, g_pre_mix, g_post_mix, g_pre_ffn, g_post_ffn, w_in, w_pool, pool_scale, g_attn_out, g_pool_out, w_out, w_router, router_bias, w_gate, w_up, w_down, w_shared_gate, w_shared_up, w_shared_down):
    batch, seq, d = x.shape
    depth = w_mod.shape[0]
    n_groups, gd = w_pool.shape[1], w_pool.shape[2]
    pool_width = n_groups * gd
    attn_width = (w_in.shape[2] - pool_width) // 3
    n_heads = attn_width // HEAD_DIM
    assert n_groups == len(POOL_WINDOWS) and seq % (ATTN_BLOCK * DILATIONS[-1]) == 0
    cosf, sinf = _rope_tables(seq)

    h = x.reshape(batch * seq, d)
    for layer in range(depth):
        mod = _modulation(c, w_mod[layer], b_mod[layer])
        proj = _in_projection(h, g_pre_mix[layer], mod, w_in[layer].astype(BF16), cosf, sinf,
                              seq, attn_width)
        attn = _dilated_attention(proj, batch, seq, n_heads)
        mixed = _mixer_norms(attn, proj, w_pool[layer].astype(BF16), pool_scale[layer],
                             g_attn_out[layer], g_pool_out[layer], batch, seq)
        y = _matmul(mixed, w_out[layer].astype(BF16), F32)
        h, fpk, idx_t, wts_t, rank_t, cum = _post_mix_route(
            y, h, mod, g_post_mix[layer], g_pre_ffn[layer], w_router[layer], router_bias[layer], seq)
        block_tables, dest_tiles = _dispatch_tables(idx_t, rank_t, cum, MOE_ROWS, COMBINE_TOKENS)
        xs = _dispatch_rows(fpk, cum, block_tables)
        ys = _routed_experts(xs, block_tables, w_gate[layer], w_up[layer], w_down[layer])
        shared = _shared_expert(fpk, w_shared_gate[layer].astype(BF16),
                                w_shared_up[layer].astype(BF16), w_shared_down[layer].astype(BF16))
        h = _combine(ys, dest_tiles, wts_t.T, shared, h, mod, g_post_ffn[layer], seq, TOP_K)
    return h.reshape(batch, seq, d)
```

```python
import functools
import math

import jax
import jax.numpy as jnp
from jax import lax
from jax.experimental import pallas as pl
from jax.experimental.pallas import tpu as pltpu

F32 = jnp.float32
BF16 = jnp.bfloat16
I32 = jnp.int32
U32 = jnp.uint32

HEAD_DIM = 128
ATTN_BLOCK = 128
DILATIONS = (1, 4, 16)
POOL_WINDOWS = (2, 4, 8, 16)
POOL_HALO = 16
ROPE_THETA = 10000.0
TOP_K = 8
N_EXPERT_GROUPS = 8
TOPK_GROUPS = 4
ROUTED_SCALE = 2.5
N_MOD = 6
NORM_EPS = 1e-6
MASK_VALUE = -1e30

LANES = 128
SUBLANES = 8
VMEM_LIMIT_BYTES = 60 * 1024 * 1024

MOE_ROWS = 256
DISPATCH_BLOCKS = 4
EXPERT_LOOKAHEAD = 2
COMBINE_TOKENS = 128

_NT = (((1,), (1,)), ((), ()))


def _params(*sem):
    return pltpu.CompilerParams(dimension_semantics=sem, vmem_limit_bytes=VMEM_LIMIT_BYTES)


def _rms(x, g):
    return x * lax.rsqrt(jnp.mean(x * x, axis=-1, keepdims=True) + NORM_EPS) * g


def _silu(x):
    return x * jax.nn.sigmoid(x)


def _pack_bf16_pair(x):
    n = x.shape[1] // 2
    lo = lax.bitcast_convert_type(x[:, :n].astype(BF16).astype(F32), U32)
    hi = lax.bitcast_convert_type(x[:, n:].astype(BF16).astype(F32), U32)
    return (lo >> 16) | hi


def _unpack_bf16_pair(w):
    lo = lax.bitcast_convert_type(w << 16, F32)
    hi = lax.bitcast_convert_type(w & jnp.uint32(0xFFFF0000), F32)
    return lo, hi


def _mod_kernel(c_ref, w_ref, b_ref, o_ref):
    cond = _silu(c_ref[...])
    o_ref[...] = jnp.dot(cond.astype(BF16), w_ref[...].astype(BF16),
                         preferred_element_type=F32) + b_ref[...]


def _modulation(c, w_mod, b_mod):
    b, d = c.shape
    n = w_mod.shape[1]
    bp = -(-b // SUBLANES) * SUBLANES
    tn = 512
    out = pl.pallas_call(
        _mod_kernel,
        grid=(n // tn,),
        in_specs=[pl.BlockSpec((bp, d), lambda j: (0, 0)),
                  pl.BlockSpec((d, tn), lambda j: (0, j)),
                  pl.BlockSpec((1, tn), lambda j: (0, j))],
        out_specs=pl.BlockSpec((bp, tn), lambda j: (0, j)),
        out_shape=jax.ShapeDtypeStruct((bp, n), F32),
        compiler_params=_params("arbitrary"),
        name="modulation",
    )(jnp.pad(c, ((0, bp - b), (0, 0))), w_mod, b_mod.reshape(1, n))
    return out[:b].reshape(b, N_MOD, d)


def _inproj_kernel(x_ref, g_ref, mod_ref, w_ref, cos_ref, sin_ref, o_ref, a_scr, *,
                   n_q_tiles, n_rope_tiles):
    j = pl.program_id(1)

    @pl.when(j == 0)
    def _():
        shift1 = mod_ref[0, 0:1, :]
        scale1 = mod_ref[0, 1:2, :]
        a = _rms(x_ref[...], g_ref[...]) * (1.0 + scale1) + shift1
        a_scr[...] = a.astype(BF16)

    acc = jnp.dot(a_scr[...], w_ref[...], preferred_element_type=F32)

    @pl.when(j < n_rope_tiles)
    def _():
        qk_scale = jnp.where(j < n_q_tiles, HEAD_DIM ** -0.5, 1.0)
        cosf = cos_ref[...] * qk_scale
        sinf = sin_ref[...] * qk_scale
        for hh in range(acc.shape[1] // HEAD_DIM):
            blk = acc[:, hh * HEAD_DIM:(hh + 1) * HEAD_DIM]
            rot = blk * cosf + pltpu.roll(blk, HEAD_DIM // 2, 1) * sinf
            o_ref[:, hh * HEAD_DIM:(hh + 1) * HEAD_DIM] = rot.astype(o_ref.dtype)

    @pl.when(j >= n_rope_tiles)
    def _():
        o_ref[...] = acc.astype(o_ref.dtype)


def _in_projection(x2, g, mod, w_in_bf, cosf, sinf, seq, attn_width):
    t, d = x2.shape
    n = w_in_bf.shape[1]
    tm = min(512, seq)
    tn = min(1024, attn_width)
    tiles_per_seq = seq // tm
    return pl.pallas_call(
        functools.partial(_inproj_kernel, n_q_tiles=attn_width // tn, n_rope_tiles=2 * attn_width // tn),
        grid=(t // tm, n // tn),
        in_specs=[pl.BlockSpec((tm, d), lambda i, j: (i, 0)),
                  pl.BlockSpec((1, d), lambda i, j: (0, 0)),
                  pl.BlockSpec((1, N_MOD, d), lambda i, j: (i // tiles_per_seq, 0, 0)),
                  pl.BlockSpec((d, tn), lambda i, j: (0, j)),
                  pl.BlockSpec((tm, HEAD_DIM), lambda i, j: (i % tiles_per_seq, 0)),
                  pl.BlockSpec((tm, HEAD_DIM), lambda i, j: (i % tiles_per_seq, 0))],
        out_specs=pl.BlockSpec((tm, tn), lambda i, j: (i, j)),
        out_shape=jax.ShapeDtypeStruct((t, n), BF16),
        scratch_shapes=[pltpu.VMEM((tm, d), BF16)],
        compiler_params=_params("arbitrary", "arbitrary"),
        name="in_projection",
    )(x2, g.reshape(1, d), mod, w_in_bf, cosf, sinf)


def _attn_kernel(q_ref, k_ref, v_ref, o_ref, qf, kf, vf, qs, ks, vs, accs, maxs, dens, stage, *, seq):
    blk = ATTN_BLOCK
    n_blk = seq // blk
    qf[...] = q_ref[...].astype(F32)
    kf[...] = k_ref[...].astype(F32)
    vf[...] = v_ref[...].astype(F32)
    vs[:, HEAD_DIM:2 * HEAD_DIM] = jnp.ones((vs.shape[0], HEAD_DIM), BF16)

    qi = lax.broadcasted_iota(I32, (blk, blk), 0)
    ki = lax.broadcasted_iota(I32, (blk, blk), 1)
    causal_bias = jnp.where(ki <= qi, 0.0, MASK_VALUE)
    upper_bias = jnp.where(ki >= qi, 0.0, MASK_VALUE)

    for bi, dil in enumerate(DILATIONS):
        sub_len = seq // dil
        nbk = sub_len // blk
        lead = blk if nbk > 1 else 0
        pitch = sub_len + lead
        def to_sub_order(res, carry, dil=dil, sub_len=sub_len, lead=lead, pitch=pitch):
            base = pl.multiple_of(jnp.asarray(res * pitch, I32), blk)
            q0 = pl.multiple_of(jnp.asarray(res * sub_len, I32), blk)
            if dil == 1:
                q_src, k_src, v_src = q_ref[...], k_ref[...], v_ref[...]
            else:
                q_src = qf[pl.ds(res, sub_len, stride=dil), :].astype(BF16)
                k_src = kf[pl.ds(res, sub_len, stride=dil), :].astype(BF16)
                v_src = vf[pl.ds(res, sub_len, stride=dil), :].astype(BF16)
            qs[pl.ds(q0, sub_len), :] = q_src
            if lead:
                ks[pl.ds(base, lead), :] = jnp.zeros((lead, HEAD_DIM), BF16)
                vs[pl.ds(base, lead), 0:HEAD_DIM] = jnp.zeros((lead, HEAD_DIM), BF16)
            ks[pl.ds(base + lead, sub_len), :] = k_src
            vs[pl.ds(base + lead, sub_len), 0:HEAD_DIM] = v_src
            return carry

        if dil == 1:
            to_sub_order(0, 0)
        else:
            lax.fori_loop(0, dil, to_sub_order, 0)

        def blocks(ref, offset, dil=dil, pitch=pitch, sub_len=sub_len, nbk=nbk):
            parts = [ref[res * pitch + offset:res * pitch + offset + sub_len, :]
                     .reshape(nbk, blk, ref.shape[1]) for res in range(dil)]
            return parts[0] if dil == 1 else jnp.concatenate(parts, axis=0)

        q3 = qs[...].reshape(n_blk, blk, HEAD_DIM)
        s_own = jnp.einsum('nqc,nkc->nqk', q3, blocks(ks, lead),
                           preferred_element_type=F32) + causal_bias[None]
        if nbk > 1:
            first = (lax.broadcasted_iota(I32, (n_blk, blk, blk), 0) & (nbk - 1)) == 0
            prev_bias = jnp.where(first, MASK_VALUE, upper_bias[None])
            s_prev = jnp.einsum('nqc,nkc->nqk', q3, blocks(ks, 0),
                                preferred_element_type=F32) + prev_bias
            m = jnp.max(jnp.maximum(s_own, s_prev), axis=-1, keepdims=True)
            accx = (jnp.einsum('nqk,nkc->nqc', jnp.exp(s_own - m).astype(BF16), blocks(vs, lead),
                               preferred_element_type=F32)
                    + jnp.einsum('nqk,nkc->nqc', jnp.exp(s_prev - m).astype(BF16), blocks(vs, 0),
                                 preferred_element_type=F32))
        else:
            m = jnp.max(s_own, axis=-1, keepdims=True)
            accx = jnp.einsum('nqk,nkc->nqc', jnp.exp(s_own - m).astype(BF16), blocks(vs, lead),
                              preferred_element_type=F32)
        m_b = jnp.broadcast_to(m, (n_blk, blk, HEAD_DIM))
        results = (accx[:, :, 0:HEAD_DIM], accx[:, :, HEAD_DIM:2 * HEAD_DIM], m_b)
        if dil == 1:
            for dst, val in zip((accs, dens, maxs), results):
                dst[bi] = val.reshape(seq, HEAD_DIM)
        else:
            for j, val in enumerate(results):
                stage[j] = val.reshape(seq, HEAD_DIM)

            def to_natural_order(res, carry, dil=dil, sub_len=sub_len, bi=bi):
                r0 = pl.multiple_of(jnp.asarray(res * sub_len, I32), blk)
                for j, dst in enumerate((accs, dens, maxs)):
                    dst[bi, pl.ds(res, sub_len, stride=dil), :] = stage[j, pl.ds(r0, sub_len), :]
                return carry

            lax.fori_loop(0, dil, to_natural_order, 0)

    def merge(c, carry):
        r0 = pl.multiple_of(c * blk, blk)
        ms = [maxs[bi, pl.ds(r0, blk), :] for bi in range(len(DILATIONS))]
        m_all = functools.reduce(jnp.maximum, ms)
        ws = [jnp.exp(m - m_all) for m in ms]
        den = sum(w * dens[bi, pl.ds(r0, blk), :] for bi, w in enumerate(ws))
        num = sum(w * accs[bi, pl.ds(r0, blk), :] for bi, w in enumerate(ws))
        o_ref[pl.ds(r0, blk), :] = (num / den).astype(o_ref.dtype)
        return carry

    lax.fori_loop(0, n_blk, merge, 0, unroll=2)


def _dilated_attention(proj, batch, seq, n_heads):
    t = proj.shape[0]
    nb = len(DILATIONS)
    key_rows = max(d * (seq // d + (ATTN_BLOCK if seq // d > ATTN_BLOCK else 0)) for d in DILATIONS)
    return pl.pallas_call(
        functools.partial(_attn_kernel, seq=seq),
        grid=(batch, n_heads),
        in_specs=[pl.BlockSpec((seq, HEAD_DIM), lambda b, h: (b, h)),
                  pl.BlockSpec((seq, HEAD_DIM), lambda b, h: (b, n_heads + h)),
                  pl.BlockSpec((seq, HEAD_DIM), lambda b, h: (b, 2 * n_heads + h))],
        out_specs=pl.BlockSpec((seq, HEAD_DIM), lambda b, h: (b, h)),
        out_shape=jax.ShapeDtypeStruct((t, n_heads * HEAD_DIM), BF16),
        scratch_shapes=[pltpu.VMEM((seq, HEAD_DIM), F32),
                        pltpu.VMEM((seq, HEAD_DIM), F32),
                        pltpu.VMEM((seq, HEAD_DIM), F32),
                        pltpu.VMEM((seq, HEAD_DIM), BF16),
                        pltpu.VMEM((key_rows, HEAD_DIM), BF16),
                        pltpu.VMEM((key_rows, 2 * HEAD_DIM), BF16),
                        pltpu.VMEM((nb, seq, HEAD_DIM), F32),
                        pltpu.VMEM((nb, seq, HEAD_DIM), F32),
                        pltpu.VMEM((nb, seq, HEAD_DIM), F32),
                        pltpu.VMEM((3, seq, HEAD_DIM), F32)],
        compiler_params=_params("arbitrary", "arbitrary"),
        name="dilated_attention",
    )(proj, proj, proj)


def _mix_kernel(attn_ref, u_ref, wp_ref, ps_ref, ga_ref, gp_ref, o_ref, ext, *, tr, aw, gd):
    j = pl.program_id(1)
    n_groups = len(POOL_WINDOWS)

    @pl.when(j == 0)
    def _():
        ext[0:POOL_HALO, :] = jnp.zeros((POOL_HALO, ext.shape[1]), BF16)

    @pl.when(j > 0)
    def _():
        ext[0:POOL_HALO, :] = ext[tr:tr + POOL_HALO, :]

    ext[POOL_HALO:POOL_HALO + tr, :] = u_ref[...]

    o_ref[:, 0:aw] = _rms(attn_ref[...].astype(F32), ga_ref[...]).astype(o_ref.dtype)

    pos = j * tr + lax.broadcasted_iota(I32, (tr, 1), 0)
    diff = (lax.broadcasted_iota(I32, (tr, tr + POOL_HALO), 0) + POOL_HALO
            - lax.broadcasted_iota(I32, (tr, tr + POOL_HALO), 1))
    ys = []
    ssq = jnp.zeros((tr, 1), F32)
    for g, w in enumerate(POOL_WINDOWS):
        band = ((diff >= 0) & (diff < w)).astype(BF16)
        win = jnp.dot(band, ext[:, g * gd:(g + 1) * gd], preferred_element_type=F32)
        cnt = jnp.minimum(pos + 1, w).astype(F32)
        mix = win * (1.0 / cnt) - u_ref[:, g * gd:(g + 1) * gd].astype(F32)
        y = jnp.dot(mix.astype(BF16), wp_ref[g], preferred_element_type=F32) * ps_ref[g:g + 1, :]
        ys.append(y)
        ssq = ssq + jnp.sum(y * y, axis=1, keepdims=True)
    r = lax.rsqrt(ssq * (1.0 / (n_groups * gd)) + NORM_EPS)
    for g in range(n_groups):
        o_ref[:, aw + g * gd:aw + (g + 1) * gd] = (
            ys[g] * r * gp_ref[:, g * gd:(g + 1) * gd]).astype(o_ref.dtype)


def _mixer_norms(attn, proj, w_pool_bf, pool_scale, g_attn, g_pool, batch, seq):
    t, aw = attn.shape
    n_groups, gd, _ = w_pool_bf.shape
    pw = n_groups * gd
    tr = min(256, seq)
    tiles = seq // tr
    u_col_block = (proj.shape[1] - pw) // pw
    return pl.pallas_call(
        functools.partial(_mix_kernel, tr=tr, aw=aw, gd=gd),
        grid=(batch, tiles),
        in_specs=[pl.BlockSpec((tr, aw), lambda b, j: (b * tiles + j, 0)),
                  pl.BlockSpec((tr, pw), lambda b, j: (b * tiles + j, u_col_block)),
                  pl.BlockSpec((n_groups, gd, gd), lambda b, j: (0, 0, 0)),
                  pl.BlockSpec((n_groups, gd), lambda b, j: (0, 0)),
                  pl.BlockSpec((1, aw), lambda b, j: (0, 0)),
                  pl.BlockSpec((1, pw), lambda b, j: (0, 0))],
        out_specs=pl.BlockSpec((tr, aw + pw), lambda b, j: (b * tiles + j, 0)),
        out_shape=jax.ShapeDtypeStruct((t, aw + pw), BF16),
        scratch_shapes=[pltpu.VMEM((tr + POOL_HALO, pw), BF16)],
        compiler_params=_params("arbitrary", "arbitrary"),
        name="mixer_norms",
    )(attn, proj, w_pool_bf, pool_scale, g_attn.reshape(1, aw), g_pool.reshape(1, pw))


def _matmul_kernel(a_ref, b_ref, o_ref):
    o_ref[...] = jnp.dot(a_ref[...], b_ref[...], preferred_element_type=F32).astype(o_ref.dtype)


def _matmul(a, b, out_dtype, tm=1024, tn=1024):
    m, k = a.shape
    n = b.shape[1]
    tm, tn = min(tm, m), min(tn, n)
    return pl.pallas_call(
        _matmul_kernel,
        grid=(m // tm, n // tn),
        in_specs=[pl.BlockSpec((tm, k), lambda i, j: (i, 0)),
                  pl.BlockSpec((k, tn), lambda i, j: (0, j))],
        out_specs=pl.BlockSpec((tm, tn), lambda i, j: (i, j)),
        out_shape=jax.ShapeDtypeStruct((m, n), out_dtype),
        compiler_params=_params("arbitrary", "arbitrary"),
        name="out_projection",
    )(a, b)


def _split_bf16(x):
    hi = x.astype(BF16)
    lo = (x - hi.astype(F32)).astype(BF16)
    return hi, lo


def _route(f, w_t, bias):
    f_hi, f_lo = _split_bf16(f)
    w_hi, w_lo = _split_bf16(w_t)
    logits = (lax.dot_general(w_hi, f_hi, _NT, preferred_element_type=F32)
              + lax.dot_general(w_hi, f_lo, _NT, preferred_element_type=F32)
              + lax.dot_general(w_lo, f_hi, _NT, preferred_element_type=F32))
    n_exp, tm = logits.shape
    per = n_exp // N_EXPERT_GROUPS
    scores = jax.nn.sigmoid(logits)
    biased = scores + bias
    ninf = -jnp.inf

    b3 = biased.reshape(N_EXPERT_GROUPS, per, tm)
    memb = lax.broadcasted_iota(I32, b3.shape, 1)
    m1 = jnp.max(b3, axis=1, keepdims=True)
    i1 = jnp.min(jnp.where(b3 == m1, memb, per), axis=1, keepdims=True)
    m2 = jnp.max(jnp.where(memb == i1, ninf, b3), axis=1, keepdims=True)
    gscore = (m1 + m2).reshape(N_EXPERT_GROUPS, tm)

    gid = lax.broadcasted_iota(I32, gscore.shape, 0)
    gmask = jnp.zeros(gscore.shape, jnp.bool_)
    cur = gscore
    for _ in range(TOPK_GROUPS):
        mx = jnp.max(cur, axis=0, keepdims=True)
        ix = jnp.min(jnp.where(cur == mx, gid, N_EXPERT_GROUPS), axis=0, keepdims=True)
        sel = gid == ix
        gmask = gmask | sel
        cur = jnp.where(sel, ninf, cur)
    emask = jnp.broadcast_to(gmask.reshape(N_EXPERT_GROUPS, 1, tm), b3.shape).reshape(n_exp, tm)

    masked = jnp.where(emask, biased, ninf)
    eid = lax.broadcasted_iota(I32, masked.shape, 0)
    idx_rows, w_rows, sels = [], [], []
    chosen = jnp.zeros(masked.shape, F32)
    for _ in range(TOP_K):
        mx = jnp.max(masked, axis=0, keepdims=True)
        ix = jnp.min(jnp.where(masked == mx, eid, n_exp), axis=0, keepdims=True)
        sel = eid == ix
        idx_rows.append(ix)
        w_rows.append(jnp.sum(jnp.where(sel, scores, 0.0), axis=0, keepdims=True))
        sels.append(sel)
        chosen = jnp.where(sel, 1.0, chosen)
        masked = jnp.where(sel, ninf, masked)
    wsel = jnp.concatenate(w_rows, axis=0)
    wts = wsel / jnp.sum(wsel, axis=0, keepdims=True) * ROUTED_SCALE
    return jnp.concatenate(idx_rows, axis=0), wts, sels, chosen


def _postmix_kernel(y_ref, x_ref, mod_ref, gpost_ref, gpre_ref, wt_ref, bias_ref,
                    h_ref, fpk_ref, idx_ref, wts_ref, rank_ref, cum_ref, carry):
    i = pl.program_id(0)

    @pl.when(i == 0)
    def _():
        carry[...] = jnp.zeros(carry.shape, F32)

    gate1 = mod_ref[0, 2:3, :]
    shift2 = mod_ref[0, 3:4, :]
    scale2 = mod_ref[0, 4:5, :]
    h = x_ref[...] + gate1 * _rms(y_ref[...], gpost_ref[...])
    h_ref[...] = h
    f = _rms(h, gpre_ref[...]) * (1.0 + scale2) + shift2
    fpk_ref[...] = _pack_bf16_pair(f)

    idx, wts, sels, chosen = _route(f, wt_ref[...], bias_ref[...])
    tm = f.shape[0]
    upper = (lax.broadcasted_iota(I32, (tm, tm), 0)
             <= lax.broadcasted_iota(I32, (tm, tm), 1)).astype(BF16)
    cum = jnp.dot(chosen.astype(BF16), upper, preferred_element_type=F32) + carry[...]
    carry[...] = cum[:, tm - 1:tm]
    idx_ref[...] = idx
    wts_ref[...] = wts
    cum_ref[...] = cum
    rank_ref[...] = jnp.concatenate(
        [jnp.sum(jnp.where(sel, cum, 0.0), axis=0, keepdims=True) for sel in sels],
        axis=0).astype(I32) - 1


def _post_mix_route(y, x2, mod, g_post, g_pre, w_router, router_bias, seq):
    t, d = x2.shape
    n_exp = w_router.shape[1]
    tm = min(256, seq)
    tiles_per_seq = seq // tm
    row = pl.BlockSpec((tm, d), lambda i: (i, 0))
    vec = pl.BlockSpec((1, d), lambda i: (0, 0))
    kt = pl.BlockSpec((TOP_K, tm), lambda i: (0, i))
    return pl.pallas_call(
        _postmix_kernel,
        grid=(t // tm,),
        in_specs=[row, row,
                  pl.BlockSpec((1, N_MOD, d), lambda i: (i // tiles_per_seq, 0, 0)),
                  vec, vec,
                  pl.BlockSpec((n_exp, d), lambda i: (0, 0)),
                  pl.BlockSpec((n_exp, 1), lambda i: (0, 0))],
        out_specs=[row, pl.BlockSpec((tm, d // 2), lambda i: (i, 0)), kt, kt, kt,
                   pl.BlockSpec((n_exp, tm), lambda i: (0, i))],
        out_shape=[jax.ShapeDtypeStruct((t, d), F32),
                   jax.ShapeDtypeStruct((t, d // 2), U32),
                   jax.ShapeDtypeStruct((TOP_K, t), I32),
                   jax.ShapeDtypeStruct((TOP_K, t), F32),
                   jax.ShapeDtypeStruct((TOP_K, t), I32),
                   jax.ShapeDtypeStruct((n_exp, t), F32)],
        scratch_shapes=[pltpu.VMEM((n_exp, 1), F32)],
        compiler_params=_params("arbitrary"),
        name="post_mix_route",
    )(y, x2, mod, g_post.reshape(1, d), g_pre.reshape(1, d), w_router.T, router_bias.reshape(n_exp, 1))


def _dispatch_tables(idx_t, rank_t, cum, rows_per_block, tokens_per_tile):
    k, t = idx_t.shape
    n_exp = cum.shape[0]
    n = k * t
    n_blocks = n // rows_per_block + n_exp
    experts = jnp.arange(n_exp, dtype=I32)
    counts = cum[:, -1].astype(I32)
    padded = (counts + rows_per_block - 1) // rows_per_block * rows_per_block
    pad_ends = jnp.cumsum(padded)
    pad_starts = pad_ends - padded
    n_used = (pad_ends[-1] // rows_per_block).astype(I32)
    n_tab = n_blocks + EXPERT_LOOKAHEAD * DISPATCH_BLOCKS
    blk_start = jnp.arange(n_tab, dtype=I32) * rows_per_block
    block_e = jnp.minimum(jnp.sum((pad_ends[None, :] <= blk_start[:, None]).astype(I32), axis=1),
                          n_exp - 1).astype(I32)
    of_block = block_e[:, None] == experts[None, :]
    first_row = blk_start - jnp.sum(jnp.where(of_block, pad_starts[None, :], 0), axis=1)
    prev_e = jnp.concatenate([jnp.full((1,), -1, I32), block_e[:-1]])
    first = (block_e != prev_e).astype(I32)
    after = jnp.sum(jnp.where(of_block, pad_ends[None, :], 0), axis=1) // rows_per_block
    next_e = jnp.where(after < n_used, block_e[jnp.minimum(after, n_tab - 1)], -1).astype(I32)
    dest = jnp.sum(jnp.where(idx_t[:, :, None] == experts[None, None, :],
                             pad_starts[None, None, :], 0), axis=2) + rank_t
    n_tiles = t // tokens_per_tile
    dest_tiles = dest.reshape(k, n_tiles, tokens_per_tile).transpose(1, 0, 2).reshape(
        n_tiles, k * tokens_per_tile).astype(I32)
    return (block_e, first, next_e, first_row.astype(I32), n_used.reshape(1)), dest_tiles


def _dispatch_kernel(be_ref, row0_ref,
                     cum_hbm, ends_hbm, fpk_hbm,
                     xs_hbm,
                     cum_v, ends_v, tok_v0, tok_v1, tok_s0, tok_s1, xs0, xs1,
                     sem_cum, sem_tok, sem_x, sem_o, *, rows, n_tokens, n_groups):
    i = pl.program_id(0)
    n_buckets = cum_v.shape[1]
    group_rows = DISPATCH_BLOCKS * rows
    tok_v, tok_s, xs = (tok_v0, tok_v1), (tok_s0, tok_s1), (xs0, xs1)

    def out_copy(slot, group):
        dst = xs_hbm.at[pl.ds(pl.multiple_of(group * group_rows, group_rows), group_rows), :]
        return pltpu.make_async_copy(xs[slot], dst, sem_o.at[slot])

    def tok_copy(slot):
        return pltpu.make_async_copy(tok_v[slot], tok_s[slot], sem_tok.at[slot])

    def invert(group, slot):
        for b in range(DISPATCH_BLOCKS):
            invert_block(group * DISPATCH_BLOCKS + b, slot, b * rows)

    def invert_block(blk, slot, col0):
        e = be_ref[blk]
        row0 = row0_ref[blk]
        ends = ends_v[pl.ds(e, 1), :]
        counts = cum_v[e]
        c_hi = jnp.floor(counts * (1.0 / LANES))
        c_lo = counts - c_hi * LANES
        if n_buckets < LANES:
            fill = jnp.zeros((LANES - n_buckets, LANES), F32)
            c_hi = jnp.concatenate([c_hi, fill], axis=0)
            c_lo = jnp.concatenate([c_lo, fill], axis=0)
        c_hi = c_hi.astype(BF16)
        c_lo = c_lo.astype(BF16)
        ones = jnp.ones((SUBLANES, LANES), BF16)
        lane = lax.broadcasted_iota(I32, (LANES, LANES), 1)
        for c in range(rows // LANES):
            j = (row0 + c * LANES + lax.broadcasted_iota(I32, (LANES, 1), 0)).astype(F32)
            done = jnp.where(ends <= j, 1.0, 0.0)
            before = jnp.where(lane == 0, 1.0, pltpu.roll(done, 1, 1))
            pick = ((1.0 - done) * before).astype(BF16)
            in_bucket = (jnp.dot(pick, c_hi, preferred_element_type=F32) * LANES
                         + jnp.dot(pick, c_lo, preferred_element_type=F32))
            below = jnp.where(in_bucket <= j, 1.0, 0.0).astype(BF16)
            n_done = lax.dot_general(ones, done.astype(BF16), _NT, preferred_element_type=F32)
            n_below = lax.dot_general(ones, below, _NT, preferred_element_type=F32)
            tok = jnp.minimum(n_done * LANES + n_below, n_tokens - 1.0)
            tok_v[slot][:, col0 + c * LANES:col0 + (c + 1) * LANES] = tok.astype(I32)

    def gather_row(slot, r, priority):
        tok = tok_s[slot][0, r]
        pltpu.async_copy(fpk_hbm.at[pl.ds(tok, 1), :], xs[slot].at[pl.ds(r, 1), :],
                         sem_x.at[slot], priority=priority)

    def gather_start(slot, unrolled):
        if unrolled:
            for r in range(group_rows):
                gather_row(slot, r, r % 2)
        else:
            def body(r, c):
                gather_row(slot, r, 0)
                return c
            lax.fori_loop(0, group_rows, body, 0, unroll=8)

    def gather_wait(slot):
        pltpu.make_async_copy(fpk_hbm.at[pl.ds(0, group_rows), :], xs[slot], sem_x.at[slot]).wait()

    @pl.when(i == 0)
    def _():
        cum_copy = pltpu.make_async_copy(cum_hbm, cum_v, sem_cum.at[0])
        ends_copy = pltpu.make_async_copy(ends_hbm, ends_v, sem_cum.at[1])
        cum_copy.start()
        ends_copy.start()
        cum_copy.wait()
        ends_copy.wait()
        invert(0, 0)
        tok_copy(0).start()
        tok_copy(0).wait()
        gather_start(0, unrolled=False)
        invert(1, 1)
        tok_copy(1).start()

    def run_group(cur):
        nxt = 1 - cur
        gather_wait(cur)
        out_copy(cur, i).start()
        tok_copy(nxt).wait()

        @pl.when(i >= 1)
        def _():
            out_copy(nxt, i - 1).wait()

        gather_start(nxt, unrolled=True)
        invert(i + 2, cur)
        tok_copy(cur).start()

        @pl.when(i == n_groups - 1)
        def _():
            gather_wait(nxt)
            tok_copy(cur).wait()
            out_copy(cur, i).wait()

    @pl.when(i % 2 == 0)
    def _():
        run_group(0)

    @pl.when(i % 2 == 1)
    def _():
        run_group(1)


def _dispatch_rows(fpk, cum, block_tables):
    block_e, _, _, first_row, _ = block_tables
    t, half = fpk.shape
    n_exp = cum.shape[0]
    rows = MOE_ROWS
    n_blocks = block_e.shape[0] - EXPERT_LOOKAHEAD * DISPATCH_BLOCKS
    assert n_blocks % DISPATCH_BLOCKS == 0
    n_groups = n_blocks // DISPATCH_BLOCKS
    group_rows = DISPATCH_BLOCKS * rows
    n_buckets = t // LANES
    assert n_buckets <= LANES, "two-level token search covers at most 128 buckets of 128 tokens"
    cum3 = cum.reshape(n_exp, n_buckets, LANES)
    ends = jnp.pad(cum3[:, :, LANES - 1], ((0, 0), (0, LANES - n_buckets)),
                   constant_values=float(2 * t))
    any_spec = pl.BlockSpec(memory_space=pl.ANY)
    grid_spec = pltpu.PrefetchScalarGridSpec(
        num_scalar_prefetch=2,
        grid=(n_groups,),
        in_specs=[any_spec, any_spec, any_spec],
        out_specs=any_spec,
        scratch_shapes=[pltpu.VMEM((n_exp, n_buckets, LANES), F32),
                        pltpu.VMEM((n_exp, LANES), F32),
                        pltpu.VMEM((SUBLANES, group_rows), I32),
                        pltpu.VMEM((SUBLANES, group_rows), I32),
                        pltpu.SMEM((SUBLANES, group_rows), I32),
                        pltpu.SMEM((SUBLANES, group_rows), I32),
                        pltpu.VMEM((group_rows, half), U32),
                        pltpu.VMEM((group_rows, half), U32),
                        pltpu.SemaphoreType.DMA((2,)),
                        pltpu.SemaphoreType.DMA((2,)),
                        pltpu.SemaphoreType.DMA((2,)),
                        pltpu.SemaphoreType.DMA((2,))])
    return pl.pallas_call(
        functools.partial(_dispatch_kernel, rows=rows, n_tokens=t, n_groups=n_groups),
        grid_spec=grid_spec,
        out_shape=jax.ShapeDtypeStruct((n_blocks * rows, half), U32),
        compiler_params=_params("arbitrary"),
        name="dispatch_rows",
    )(block_e, first_row, cum3, ends, fpk)


def _expert_kernel(be_ref, first_ref, nexte_ref, nused_ref,
                   x_ref, wg_hbm, wu_hbm, wd_hbm,
                   o_ref,
                   stage_g, stage_u, stage_d, wg, wu, wd, sem_w):
    i = pl.program_id(0)
    n_used = nused_ref[0]
    half = wd.shape[1] // 2

    def weight_pairs(e):
        return ((wg_hbm.at[e], stage_g), (wu_hbm.at[e], stage_u), (wd_hbm.at[e], stage_d))

    def weights_start(e):
        for k, (src, dst) in enumerate(weight_pairs(e)):
            pltpu.async_copy(src, dst, sem_w.at[k], priority=1)

    def weights_wait(e):
        for k, (src, dst) in enumerate(weight_pairs(e)):
            pltpu.make_async_copy(src, dst, sem_w.at[k]).wait()

    def cast_weights(src, dst):
        chunk = math.gcd(256, src.shape[0])
        def body(c, carry):
            r0 = pl.multiple_of(c * chunk, chunk)
            dst[pl.ds(r0, chunk), :] = src[pl.ds(r0, chunk), :].astype(BF16)
            return carry
        lax.fori_loop(0, src.shape[0] // chunk, body, 0)

    @pl.when((i == 0) & (n_used > 0))
    def _():
        weights_start(be_ref[0])

    @pl.when(i < n_used)
    def _():
        @pl.when(first_ref[i] == 1)
        def _():
            weights_wait(be_ref[i])
            cast_weights(stage_g, wg)
            cast_weights(stage_u, wu)
            cast_weights(stage_d, wd)

            @pl.when(nexte_ref[i] >= 0)
            def _():
                weights_start(nexte_ref[i])

        lo, hi = _unpack_bf16_pair(x_ref[...])
        x_lo = lo.astype(BF16)
        x_hi = hi.astype(BF16)
        g = (jnp.dot(x_lo, wg[0:half, :], preferred_element_type=F32)
             + jnp.dot(x_hi, wg[half:2 * half, :], preferred_element_type=F32))
        u = (jnp.dot(x_lo, wu[0:half, :], preferred_element_type=F32)
             + jnp.dot(x_hi, wu[half:2 * half, :], preferred_element_type=F32))
        act = (_silu(g) * u).astype(BF16)
        o_ref[...] = _pack_bf16_pair(jnp.dot(act, wd[...], preferred_element_type=F32))

    @pl.when(i >= n_used)
    def _():
        o_ref[...] = jnp.zeros(o_ref.shape, o_ref.dtype)


def _routed_experts(xs, block_tables, w_gate, w_up, w_down):
    block_e, first, next_e, _, n_used = block_tables
    n_rows, half = xs.shape
    d = 2 * half
    n_exp, _, de = w_gate.shape
    rows = MOE_ROWS
    n_blocks = n_rows // rows
    any_spec = pl.BlockSpec(memory_space=pl.ANY)
    grid_spec = pltpu.PrefetchScalarGridSpec(
        num_scalar_prefetch=4,
        grid=(n_blocks,),
        in_specs=[pl.BlockSpec((rows, half), lambda i, *_: (i, 0)), any_spec, any_spec, any_spec],
        out_specs=pl.BlockSpec((rows, half), lambda i, *_: (i, 0)),
        scratch_shapes=[pltpu.VMEM((d, de), F32),
                        pltpu.VMEM((d, de), F32),
                        pltpu.VMEM((de, d), F32),
                        pltpu.VMEM((d, de), BF16),
                        pltpu.VMEM((d, de), BF16),
                        pltpu.VMEM((de, d), BF16),
                        pltpu.SemaphoreType.DMA((3,))])
    return pl.pallas_call(
        _expert_kernel,
        grid_spec=grid_spec,
        out_shape=jax.ShapeDtypeStruct((n_rows, half), U32),
        compiler_params=_params("arbitrary"),
        name="routed_experts",
    )(block_e, first, next_e, n_used, xs, w_gate, w_up, w_down)


def _shared_kernel(fpk_ref, wg_ref, wu_ref, wd_ref, o_ref):
    half = fpk_ref.shape[1]
    lo, hi = _unpack_bf16_pair(fpk_ref[...])
    x_lo = lo.astype(BF16)
    x_hi = hi.astype(BF16)
    g = (jnp.dot(x_lo, wg_ref[0:half, :], preferred_element_type=F32)
         + jnp.dot(x_hi, wg_ref[half:2 * half, :], preferred_element_type=F32))
    u = (jnp.dot(x_lo, wu_ref[0:half, :], preferred_element_type=F32)
         + jnp.dot(x_hi, wu_ref[half:2 * half, :], preferred_element_type=F32))
    act = (_silu(g) * u).astype(BF16)
    o_ref[...] = jnp.dot(act, wd_ref[...], preferred_element_type=F32).astype(o_ref.dtype)


def _shared_expert(fpk, wg_bf, wu_bf, wd_bf):
    t, half = fpk.shape
    d = 2 * half
    ds = wg_bf.shape[1]
    tm = min(512, t)
    return pl.pallas_call(
        _shared_kernel,
        grid=(t // tm,),
        in_specs=[pl.BlockSpec((tm, half), lambda i: (i, 0)),
                  pl.BlockSpec((d, ds), lambda i: (0, 0)),
                  pl.BlockSpec((d, ds), lambda i: (0, 0)),
                  pl.BlockSpec((ds, d), lambda i: (0, 0))],
        out_specs=pl.BlockSpec((tm, d), lambda i: (i, 0)),
        out_shape=jax.ShapeDtypeStruct((t, d), BF16),
        compiler_params=_params("arbitrary"),
        name="shared_expert",
    )(fpk, wg_bf, wu_bf, wd_bf)


def _combine_kernel(dest_hbm, ys_hbm, wts_ref, sh_ref, h_ref, mod_ref, g_ref, o_ref,
                    dest_s0, dest_s1, gbuf0, gbuf1, sem_d, sem_g, *, tm, top_k, n_tiles):
    i = pl.program_id(0)
    n_rows = top_k * tm
    half = gbuf0.shape[1]
    dest_s, gbuf = (dest_s0, dest_s1), (gbuf0, gbuf1)
    last = n_tiles - 1

    def dest_copy(tile, s):
        return pltpu.make_async_copy(dest_hbm.at[tile], dest_s[s], sem_d.at[s])

    def gather_row(s, r, priority):
        row = dest_s[s][r]
        pltpu.async_copy(ys_hbm.at[pl.ds(row, 1), :], gbuf[s].at[pl.ds(r, 1), :], sem_g.at[s],
                         priority=priority)

    def gather_start(s, unrolled):
        if unrolled:
            for r in range(n_rows):
                gather_row(s, r, r % 2)
        else:
            def body(r, c):
                gather_row(s, r, 0)
                return c
            lax.fori_loop(0, n_rows, body, 0, unroll=8)

    def gather_wait(s):
        pltpu.make_async_copy(ys_hbm.at[pl.ds(0, n_rows), :], gbuf[s], sem_g.at[s]).wait()

    @pl.when(i == 0)
    def _():
        first_dest = dest_copy(0, 0)
        first_dest.start()
        first_dest.wait()
        gather_start(0, unrolled=False)
        dest_copy(jnp.minimum(1, last), 1).start()

    def run_tile(cur):
        nxt = 1 - cur
        gather_wait(cur)
        dest_copy(jnp.minimum(i + 1, last), nxt).wait()
        gather_start(nxt, unrolled=True)
        dest_copy(jnp.minimum(i + 2, last), cur).start()
        acc_lo = jnp.zeros((tm, half), F32)
        acc_hi = jnp.zeros((tm, half), F32)
        for k in range(top_k):
            lo, hi = _unpack_bf16_pair(gbuf[cur][k * tm:(k + 1) * tm, :])
            w = wts_ref[:, k:k + 1]
            acc_lo = acc_lo + w * lo
            acc_hi = acc_hi + w * hi
        moe_lo = acc_lo + sh_ref[:, 0:half].astype(F32)
        moe_hi = acc_hi + sh_ref[:, half:2 * half].astype(F32)
        ssq = (jnp.sum(moe_lo * moe_lo, axis=1, keepdims=True)
               + jnp.sum(moe_hi * moe_hi, axis=1, keepdims=True))
        r = lax.rsqrt(ssq * (1.0 / (2 * half)) + NORM_EPS)
        gate2 = mod_ref[0, 5:6, :]
        g = g_ref[...]
        o_ref[:, 0:half] = h_ref[:, 0:half] + gate2[:, 0:half] * (moe_lo * r * g[:, 0:half])
        o_ref[:, half:2 * half] = (h_ref[:, half:2 * half]
                                   + gate2[:, half:2 * half] * (moe_hi * r * g[:, half:2 * half]))

        @pl.when(i == last)
        def _():
            gather_wait(nxt)
            dest_copy(last, cur).wait()

    @pl.when(i % 2 == 0)
    def _():
        run_tile(0)

    @pl.when(i % 2 == 1)
    def _():
        run_tile(1)


def _combine(ys, dest_tiles, wts, shared, h, mod, g_post, seq, top_k):
    t, d = h.shape
    tm = COMBINE_TOKENS
    n_tiles = t // tm
    tiles_per_seq = seq // tm
    row = pl.BlockSpec((tm, d), lambda i: (i, 0))
    any_spec = pl.BlockSpec(memory_space=pl.ANY)
    return pl.pallas_call(
        functools.partial(_combine_kernel, tm=tm, top_k=top_k, n_tiles=n_tiles),
        grid=(n_tiles,),
        in_specs=[any_spec, any_spec,
                  pl.BlockSpec((tm, top_k), lambda i: (i, 0)),
                  row, row,
                  pl.BlockSpec((1, N_MOD, d), lambda i: (i // tiles_per_seq, 0, 0)),
                  pl.BlockSpec((1, d), lambda i: (0, 0))],
        out_specs=row,
        out_shape=jax.ShapeDtypeStruct((t, d), F32),
        scratch_shapes=[pltpu.SMEM((top_k * tm,), I32),
                        pltpu.SMEM((top_k * tm,), I32),
                        pltpu.VMEM((top_k * tm, d // 2), U32),
                        pltpu.VMEM((top_k * tm, d // 2), U32),
                        pltpu.SemaphoreType.DMA((2,)),
                        pltpu.SemaphoreType.DMA((2,))],
        compiler_params=_params("arbitrary"),
        name="combine",
    )(dest_tiles, ys, wts, shared, h, mod, g_post.reshape(1, d))


def _rope_tables(seq):
    half = HEAD_DIM // 2
    inv_freq = jnp.exp(-math.log(ROPE_THETA) * jnp.arange(half, dtype=F32) / half)
    ang = jnp.arange(seq, dtype=F32)[:, None] * inv_freq[None, :]
    cos, sin = jnp.cos(ang), jnp.sin(ang)
    return jnp.concatenate([cos, cos], axis=1), jnp.concatenate([-sin, sin], axis=1)


def kernel(x, c, w_mod, b_mod, g_pre_mix, g_post_mix, g_pre_ffn, g_post_ffn, w_in, w_pool, pool_scale, g_attn_out, g_pool_out, w_out, w_router, router_bias, w_gate, w_up, w_down, w_shared_gate, w_shared_up, w_shared_down):
    batch, seq, d = x.shape
    depth = w_mod.shape[0]
    n_groups, gd = w_pool.shape[1], w_pool.shape[2]
    pool_width = n_groups * gd
    attn_width = (w_in.shape[2] - pool_width) // 3
    n_heads = attn_width // HEAD_DIM
    assert n_groups == len(POOL_WINDOWS) and seq % (ATTN_BLOCK * DILATIONS[-1]) == 0
    cosf, sinf = _rope_tables(seq)

    h = x.reshape(batch * seq, d)
    for layer in range(depth):
        mod = _modulation(c, w_mod[layer], b_mod[layer])
        proj = _in_projection(h, g_pre_mix[layer], mod, w_in[layer].astype(BF16), cosf, sinf,
                              seq, attn_width)
        attn = _dilated_attention(proj, batch, seq, n_heads)
        mixed = _mixer_norms(attn, proj, w_pool[layer].astype(BF16), pool_scale[layer],
                             g_attn_out[layer], g_pool_out[layer], batch, seq)
        y = _matmul(mixed, w_out[layer].astype(BF16), F32)
        h, fpk, idx_t, wts_t, rank_t, cum = _post_mix_route(
            y, h, mod, g_post_mix[layer], g_pre_ffn[layer], w_router[layer], router_bias[layer], seq)
        block_tables, dest_tiles = _dispatch_tables(idx_t, rank_t, cum, MOE_ROWS, COMBINE_TOKENS)
        xs = _dispatch_rows(fpk, cum, block_tables)
        ys = _routed_experts(xs, block_tables, w_gate[layer], w_up[layer], w_down[layer])
        shared = _shared_expert(fpk, w_shared_gate[layer].astype(BF16),
                                w_shared_up[layer].astype(BF16), w_shared_down[layer].astype(BF16))
        h = _combine(ys, dest_tiles, wts_t.T, shared, h, mod, g_post_ffn[layer], seq, TOP_K)
    return h.reshape(batch, seq, d)
```

```python
import functools
import math

import jax
import jax.numpy as jnp
from jax import lax
from jax.experimental import pallas as pl
from jax.experimental.pallas import tpu as pltpu

F32 = jnp.float32
BF16 = jnp.bfloat16
I32 = jnp.int32
U32 = jnp.uint32

HEAD_DIM = 128
ATTN_BLOCK = 128
DILATIONS = (1, 4, 16)
POOL_WINDOWS = (2, 4, 8, 16)
POOL_HALO = 16
ROPE_THETA = 10000.0
TOP_K = 8
N_EXPERT_GROUPS = 8
TOPK_GROUPS = 4
ROUTED_SCALE = 2.5
N_MOD = 6
NORM_EPS = 1e-6
MASK_VALUE = -1e30

LANES = 128
SUBLANES = 8
VMEM_LIMIT_BYTES = 60 * 1024 * 1024

MOE_ROWS = 256
EXPERT_LOOKAHEAD = 2
COMBINE_TOKENS = 128

_NT = (((1,), (1,)), ((), ()))


def _params(*sem):
    return pltpu.CompilerParams(dimension_semantics=sem, vmem_limit_bytes=VMEM_LIMIT_BYTES)


def _rms(x, g):
    return x * lax.rsqrt(jnp.mean(x * x, axis=-1, keepdims=True) + NORM_EPS) * g


def _silu(x):
    return x * jax.nn.sigmoid(x)


def _pack_bf16_pair(x):
    n = x.shape[1] // 2
    lo = lax.bitcast_convert_type(x[:, :n].astype(BF16).astype(F32), U32)
    hi = lax.bitcast_convert_type(x[:, n:].astype(BF16).astype(F32), U32)
    return (lo >> 16) | hi


def _unpack_bf16_pair(w):
    lo = lax.bitcast_convert_type(w << 16, F32)
    hi = lax.bitcast_convert_type(w & jnp.uint32(0xFFFF0000), F32)
    return lo, hi


def _mod_kernel(c_ref, w_ref, b_ref, o_ref):
    cond = _silu(c_ref[...])
    o_ref[...] = jnp.dot(cond.astype(BF16), w_ref[...].astype(BF16),
                         preferred_element_type=F32) + b_ref[...]


def _modulation(c, w_mod, b_mod):
    b, d = c.shape
    n = w_mod.shape[1]
    bp = -(-b // SUBLANES) * SUBLANES
    tn = 512
    out = pl.pallas_call(
        _mod_kernel,
        grid=(n // tn,),
        in_specs=[pl.BlockSpec((bp, d), lambda j: (0, 0)),
                  pl.BlockSpec((d, tn), lambda j: (0, j)),
                  pl.BlockSpec((1, tn), lambda j: (0, j))],
        out_specs=pl.BlockSpec((bp, tn), lambda j: (0, j)),
        out_shape=jax.ShapeDtypeStruct((bp, n), F32),
        compiler_params=_params("arbitrary"),
        name="modulation",
    )(jnp.pad(c, ((0, bp - b), (0, 0))), w_mod, b_mod.reshape(1, n))
    return out[:b].reshape(b, N_MOD, d)


def _inproj_kernel(x_ref, g_ref, mod_ref, w_ref, cos_ref, sin_ref, o_ref, a_scr, *,
                   n_q_tiles, n_rope_tiles):
    j = pl.program_id(1)

    @pl.when(j == 0)
    def _():
        shift1 = mod_ref[0, 0:1, :]
        scale1 = mod_ref[0, 1:2, :]
        a = _rms(x_ref[...], g_ref[...]) * (1.0 + scale1) + shift1
        a_scr[...] = a.astype(BF16)

    acc = jnp.dot(a_scr[...], w_ref[...], preferred_element_type=F32)

    @pl.when(j < n_rope_tiles)
    def _():
        qk_scale = jnp.where(j < n_q_tiles, HEAD_DIM ** -0.5, 1.0)
        cosf = cos_ref[...] * qk_scale
        sinf = sin_ref[...] * qk_scale
        for hh in range(acc.shape[1] // HEAD_DIM):
            blk = acc[:, hh * HEAD_DIM:(hh + 1) * HEAD_DIM]
            rot = blk * cosf + pltpu.roll(blk, HEAD_DIM // 2, 1) * sinf
            o_ref[:, hh * HEAD_DIM:(hh + 1) * HEAD_DIM] = rot.astype(o_ref.dtype)

    @pl.when(j >= n_rope_tiles)
    def _():
        o_ref[...] = acc.astype(o_ref.dtype)


def _in_projection(x2, g, mod, w_in_bf, cosf, sinf, seq, attn_width):
    t, d = x2.shape
    n = w_in_bf.shape[1]
    tm = min(512, seq)
    tn = min(1024, attn_width)
    tiles_per_seq = seq // tm
    return pl.pallas_call(
        functools.partial(_inproj_kernel, n_q_tiles=attn_width // tn, n_rope_tiles=2 * attn_width // tn),
        grid=(t // tm, n // tn),
        in_specs=[pl.BlockSpec((tm, d), lambda i, j: (i, 0)),
                  pl.BlockSpec((1, d), lambda i, j: (0, 0)),
                  pl.BlockSpec((1, N_MOD, d), lambda i, j: (i // tiles_per_seq, 0, 0)),
                  pl.BlockSpec((d, tn), lambda i, j: (0, j)),
                  pl.BlockSpec((tm, HEAD_DIM), lambda i, j: (i % tiles_per_seq, 0)),
                  pl.BlockSpec((tm, HEAD_DIM), lambda i, j: (i % tiles_per_seq, 0))],
        out_specs=pl.BlockSpec((tm, tn), lambda i, j: (i, j)),
        out_shape=jax.ShapeDtypeStruct((t, n), BF16),
        scratch_shapes=[pltpu.VMEM((tm, d), BF16)],
        compiler_params=_params("arbitrary", "arbitrary"),
        name="in_projection",
    )(x2, g.reshape(1, d), mod, w_in_bf, cosf, sinf)


def _attn_kernel(q_ref, k_ref, v_ref, o_ref, qf, kf, vf, qs, ks, vs, accs, maxs, dens, stage, *, seq):
    blk = ATTN_BLOCK
    n_blk = seq // blk
    qf[...] = q_ref[...].astype(F32)
    kf[...] = k_ref[...].astype(F32)
    vf[...] = v_ref[...].astype(F32)
    vs[:, HEAD_DIM:2 * HEAD_DIM] = jnp.ones((vs.shape[0], HEAD_DIM), BF16)

    qi = lax.broadcasted_iota(I32, (blk, blk), 0)
    ki = lax.broadcasted_iota(I32, (blk, blk), 1)
    causal_bias = jnp.where(ki <= qi, 0.0, MASK_VALUE)
    upper_bias = jnp.where(ki >= qi, 0.0, MASK_VALUE)

    for bi, dil in enumerate(DILATIONS):
        sub_len = seq // dil
        nbk = sub_len // blk
        lead = blk if nbk > 1 else 0
        pitch = sub_len + lead
        def to_sub_order(res, carry, dil=dil, sub_len=sub_len, lead=lead, pitch=pitch):
            base = pl.multiple_of(jnp.asarray(res * pitch, I32), blk)
            q0 = pl.multiple_of(jnp.asarray(res * sub_len, I32), blk)
            if dil == 1:
                q_src, k_src, v_src = q_ref[...], k_ref[...], v_ref[...]
            else:
                q_src = qf[pl.ds(res, sub_len, stride=dil), :].astype(BF16)
                k_src = kf[pl.ds(res, sub_len, stride=dil), :].astype(BF16)
                v_src = vf[pl.ds(res, sub_len, stride=dil), :].astype(BF16)
            qs[pl.ds(q0, sub_len), :] = q_src
            if lead:
                ks[pl.ds(base, lead), :] = jnp.zeros((lead, HEAD_DIM), BF16)
                vs[pl.ds(base, lead), 0:HEAD_DIM] = jnp.zeros((lead, HEAD_DIM), BF16)
            ks[pl.ds(base + lead, sub_len), :] = k_src
            vs[pl.ds(base + lead, sub_len), 0:HEAD_DIM] = v_src
            return carry

        if dil == 1:
            to_sub_order(0, 0)
        else:
            lax.fori_loop(0, dil, to_sub_order, 0)

        def blocks(ref, offset, dil=dil, pitch=pitch, sub_len=sub_len, nbk=nbk):
            parts = [ref[res * pitch + offset:res * pitch + offset + sub_len, :]
                     .reshape(nbk, blk, ref.shape[1]) for res in range(dil)]
            return parts[0] if dil == 1 else jnp.concatenate(parts, axis=0)

        q3 = qs[...].reshape(n_blk, blk, HEAD_DIM)
        s_own = jnp.einsum('nqc,nkc->nqk', q3, blocks(ks, lead),
                           preferred_element_type=F32) + causal_bias[None]
        if nbk > 1:
            first = (lax.broadcasted_iota(I32, (n_blk, blk, blk), 0) & (nbk - 1)) == 0
            prev_bias = jnp.where(first, MASK_VALUE, upper_bias[None])
            s_prev = jnp.einsum('nqc,nkc->nqk', q3, blocks(ks, 0),
                                preferred_element_type=F32) + prev_bias
            m = jnp.max(jnp.maximum(s_own, s_prev), axis=-1, keepdims=True)
            accx = (jnp.einsum('nqk,nkc->nqc', jnp.exp(s_own - m).astype(BF16), blocks(vs, lead),
                               preferred_element_type=F32)
                    + jnp.einsum('nqk,nkc->nqc', jnp.exp(s_prev - m).astype(BF16), blocks(vs, 0),
                                 preferred_element_type=F32))
        else:
            m = jnp.max(s_own, axis=-1, keepdims=True)
            accx = jnp.einsum('nqk,nkc->nqc', jnp.exp(s_own - m).astype(BF16), blocks(vs, lead),
                              preferred_element_type=F32)
        m_b = jnp.broadcast_to(m, (n_blk, blk, HEAD_DIM))
        results = (accx[:, :, 0:HEAD_DIM], accx[:, :, HEAD_DIM:2 * HEAD_DIM], m_b)
        if dil == 1:
            for dst, val in zip((accs, dens, maxs), results):
                dst[bi] = val.reshape(seq, HEAD_DIM)
        else:
            for j, val in enumerate(results):
                stage[j] = val.reshape(seq, HEAD_DIM)

            def to_natural_order(res, carry, dil=dil, sub_len=sub_len, bi=bi):
                r0 = pl.multiple_of(jnp.asarray(res * sub_len, I32), blk)
                for j, dst in enumerate((accs, dens, maxs)):
                    dst[bi, pl.ds(res, sub_len, stride=dil), :] = stage[j, pl.ds(r0, sub_len), :]
                return carry

            lax.fori_loop(0, dil, to_natural_order, 0)

    def merge(c, carry):
        r0 = pl.multiple_of(c * blk, blk)
        ms = [maxs[bi, pl.ds(r0, blk), :] for bi in range(len(DILATIONS))]
        m_all = functools.reduce(jnp.maximum, ms)
        ws = [jnp.exp(m - m_all) for m in ms]
        den = sum(w * dens[bi, pl.ds(r0, blk), :] for bi, w in enumerate(ws))
        num = sum(w * accs[bi, pl.ds(r0, blk), :] for bi, w in enumerate(ws))
        o_ref[pl.ds(r0, blk), :] = (num / den).astype(o_ref.dtype)
        return carry

    lax.fori_loop(0, n_blk, merge, 0, unroll=2)


def _dilated_attention(proj, batch, seq, n_heads):
    t = proj.shape[0]
    nb = len(DILATIONS)
    key_rows = max(d * (seq // d + (ATTN_BLOCK if seq // d > ATTN_BLOCK else 0)) for d in DILATIONS)
    return pl.pallas_call(
        functools.partial(_attn_kernel, seq=seq),
        grid=(batch, n_heads),
        in_specs=[pl.BlockSpec((seq, HEAD_DIM), lambda b, h: (b, h)),
                  pl.BlockSpec((seq, HEAD_DIM), lambda b, h: (b, n_heads + h)),
                  pl.BlockSpec((seq, HEAD_DIM), lambda b, h: (b, 2 * n_heads + h))],
        out_specs=pl.BlockSpec((seq, HEAD_DIM), lambda b, h: (b, h)),
        out_shape=jax.ShapeDtypeStruct((t, n_heads * HEAD_DIM), BF16),
        scratch_shapes=[pltpu.VMEM((seq, HEAD_DIM), F32),
                        pltpu.VMEM((seq, HEAD_DIM), F32),
                        pltpu.VMEM((seq, HEAD_DIM), F32),
                        pltpu.VMEM((seq, HEAD_DIM), BF16),
                        pltpu.VMEM((key_rows, HEAD_DIM), BF16),
                        pltpu.VMEM((key_rows, 2 * HEAD_DIM), BF16),
                        pltpu.VMEM((nb, seq, HEAD_DIM), F32),
                        pltpu.VMEM((nb, seq, HEAD_DIM), F32),
                        pltpu.VMEM((nb, seq, HEAD_DIM), F32),
                        pltpu.VMEM((3, seq, HEAD_DIM), F32)],
        compiler_params=_params("arbitrary", "arbitrary"),
        name="dilated_attention",
    )(proj, proj, proj)


def _mix_kernel(attn_ref, u_ref, wp_ref, ps_ref, ga_ref, gp_ref, o_ref, ext, *, tr, aw, gd):
    j = pl.program_id(1)
    n_groups = len(POOL_WINDOWS)

    @pl.when(j == 0)
    def _():
        ext[0:POOL_HALO, :] = jnp.zeros((POOL_HALO, ext.shape[1]), BF16)

    @pl.when(j > 0)
    def _():
        ext[0:POOL_HALO, :] = ext[tr:tr + POOL_HALO, :]

    ext[POOL_HALO:POOL_HALO + tr, :] = u_ref[...]

    o_ref[:, 0:aw] = _rms(attn_ref[...].astype(F32), ga_ref[...]).astype(o_ref.dtype)

    pos = j * tr + lax.broadcasted_iota(I32, (tr, 1), 0)
    diff = (lax.broadcasted_iota(I32, (tr, tr + POOL_HALO), 0) + POOL_HALO
            - lax.broadcasted_iota(I32, (tr, tr + POOL_HALO), 1))
    ys = []
    ssq = jnp.zeros((tr, 1), F32)
    for g, w in enumerate(POOL_WINDOWS):
        band = ((diff >= 0) & (diff < w)).astype(BF16)
        win = jnp.dot(band, ext[:, g * gd:(g + 1) * gd], preferred_element_type=F32)
        cnt = jnp.minimum(pos + 1, w).astype(F32)
        mix = win * (1.0 / cnt) - u_ref[:, g * gd:(g + 1) * gd].astype(F32)
        y = jnp.dot(mix.astype(BF16), wp_ref[g], preferred_element_type=F32) * ps_ref[g:g + 1, :]
        ys.append(y)
        ssq = ssq + jnp.sum(y * y, axis=1, keepdims=True)
    r = lax.rsqrt(ssq * (1.0 / (n_groups * gd)) + NORM_EPS)
    for g in range(n_groups):
        o_ref[:, aw + g * gd:aw + (g + 1) * gd] = (
            ys[g] * r * gp_ref[:, g * gd:(g + 1) * gd]).astype(o_ref.dtype)


def _mixer_norms(attn, proj, w_pool_bf, pool_scale, g_attn, g_pool, batch, seq):
    t, aw = attn.shape
    n_groups, gd, _ = w_pool_bf.shape
    pw = n_groups * gd
    tr = min(256, seq)
    tiles = seq // tr
    u_col_block = (proj.shape[1] - pw) // pw
    return pl.pallas_call(
        functools.partial(_mix_kernel, tr=tr, aw=aw, gd=gd),
        grid=(batch, tiles),
        in_specs=[pl.BlockSpec((tr, aw), lambda b, j: (b * tiles + j, 0)),
                  pl.BlockSpec((tr, pw), lambda b, j: (b * tiles + j, u_col_block)),
                  pl.BlockSpec((n_groups, gd, gd), lambda b, j: (0, 0, 0)),
                  pl.BlockSpec((n_groups, gd), lambda b, j: (0, 0)),
                  pl.BlockSpec((1, aw), lambda b, j: (0, 0)),
                  pl.BlockSpec((1, pw), lambda b, j: (0, 0))],
        out_specs=pl.BlockSpec((tr, aw + pw), lambda b, j: (b * tiles + j, 0)),
        out_shape=jax.ShapeDtypeStruct((t, aw + pw), BF16),
        scratch_shapes=[pltpu.VMEM((tr + POOL_HALO, pw), BF16)],
        compiler_params=_params("arbitrary", "arbitrary"),
        name="mixer_norms",
    )(attn, proj, w_pool_bf, pool_scale, g_attn.reshape(1, aw), g_pool.reshape(1, pw))


def _matmul_kernel(a_ref, b_ref, o_ref):
    o_ref[...] = jnp.dot(a_ref[...], b_ref[...], preferred_element_type=F32).astype(o_ref.dtype)


def _matmul(a, b, out_dtype, tm=1024, tn=1024):
    m, k = a.shape
    n = b.shape[1]
    tm, tn = min(tm, m), min(tn, n)
    return pl.pallas_call(
        _matmul_kernel,
        grid=(m // tm, n // tn),
        in_specs=[pl.BlockSpec((tm, k), lambda i, j: (i, 0)),
                  pl.BlockSpec((k, tn), lambda i, j: (0, j))],
        out_specs=pl.BlockSpec((tm, tn), lambda i, j: (i, j)),
        out_shape=jax.ShapeDtypeStruct((m, n), out_dtype),
        compiler_params=_params("arbitrary", "arbitrary"),
        name="out_projection",
    )(a, b)


def _split_bf16(x):
    hi = x.astype(BF16)
    lo = (x - hi.astype(F32)).astype(BF16)
    return hi, lo


def _route(f, w_t, bias):
    f_hi, f_lo = _split_bf16(f)
    w_hi, w_lo = _split_bf16(w_t)
    logits = (lax.dot_general(w_hi, f_hi, _NT, preferred_element_type=F32)
              + lax.dot_general(w_hi, f_lo, _NT, preferred_element_type=F32)
              + lax.dot_general(w_lo, f_hi, _NT, preferred_element_type=F32))
    n_exp, tm = logits.shape
    per = n_exp // N_EXPERT_GROUPS
    scores = jax.nn.sigmoid(logits)
    biased = scores + bias
    ninf = -jnp.inf

    b3 = biased.reshape(N_EXPERT_GROUPS, per, tm)
    memb = lax.broadcasted_iota(I32, b3.shape, 1)
    m1 = jnp.max(b3, axis=1, keepdims=True)
    i1 = jnp.min(jnp.where(b3 == m1, memb, per), axis=1, keepdims=True)
    m2 = jnp.max(jnp.where(memb == i1, ninf, b3), axis=1, keepdims=True)
    gscore = (m1 + m2).reshape(N_EXPERT_GROUPS, tm)

    gid = lax.broadcasted_iota(I32, gscore.shape, 0)
    gmask = jnp.zeros(gscore.shape, jnp.bool_)
    cur = gscore
    for _ in range(TOPK_GROUPS):
        mx = jnp.max(cur, axis=0, keepdims=True)
        ix = jnp.min(jnp.where(cur == mx, gid, N_EXPERT_GROUPS), axis=0, keepdims=True)
        sel = gid == ix
        gmask = gmask | sel
        cur = jnp.where(sel, ninf, cur)
    emask = jnp.broadcast_to(gmask.reshape(N_EXPERT_GROUPS, 1, tm), b3.shape).reshape(n_exp, tm)

    masked = jnp.where(emask, biased, ninf)
    eid = lax.broadcasted_iota(I32, masked.shape, 0)
    idx_rows, w_rows, sels = [], [], []
    chosen = jnp.zeros(masked.shape, F32)
    for _ in range(TOP_K):
        mx = jnp.max(masked, axis=0, keepdims=True)
        ix = jnp.min(jnp.where(masked == mx, eid, n_exp), axis=0, keepdims=True)
        sel = eid == ix
        idx_rows.append(ix)
        w_rows.append(jnp.sum(jnp.where(sel, scores, 0.0), axis=0, keepdims=True))
        sels.append(sel)
        chosen = jnp.where(sel, 1.0, chosen)
        masked = jnp.where(sel, ninf, masked)
    wsel = jnp.concatenate(w_rows, axis=0)
    wts = wsel / jnp.sum(wsel, axis=0, keepdims=True) * ROUTED_SCALE
    return jnp.concatenate(idx_rows, axis=0), wts, sels, chosen


def _postmix_kernel(y_ref, x_ref, mod_ref, gpost_ref, gpre_ref, wt_ref, bias_ref,
                    h_ref, fpk_ref, idx_ref, wts_ref, rank_ref, cum_ref, carry):
    i = pl.program_id(0)

    @pl.when(i == 0)
    def _():
        carry[...] = jnp.zeros(carry.shape, F32)

    gate1 = mod_ref[0, 2:3, :]
    shift2 = mod_ref[0, 3:4, :]
    scale2 = mod_ref[0, 4:5, :]
    h = x_ref[...] + gate1 * _rms(y_ref[...], gpost_ref[...])
    h_ref[...] = h
    f = _rms(h, gpre_ref[...]) * (1.0 + scale2) + shift2
    fpk_ref[...] = _pack_bf16_pair(f)

    idx, wts, sels, chosen = _route(f, wt_ref[...], bias_ref[...])
    tm = f.shape[0]
    upper = (lax.broadcasted_iota(I32, (tm, tm), 0)
             <= lax.broadcasted_iota(I32, (tm, tm), 1)).astype(BF16)
    cum = jnp.dot(chosen.astype(BF16), upper, preferred_element_type=F32) + carry[...]
    carry[...] = cum[:, tm - 1:tm]
    idx_ref[...] = idx
    wts_ref[...] = wts
    cum_ref[...] = cum
    rank_ref[...] = jnp.concatenate(
        [jnp.sum(jnp.where(sel, cum, 0.0), axis=0, keepdims=True) for sel in sels],
        axis=0).astype(I32) - 1


def _post_mix_route(y, x2, mod, g_post, g_pre, w_router, router_bias, seq):
    t, d = x2.shape
    n_exp = w_router.shape[1]
    tm = min(256, seq)
    tiles_per_seq = seq // tm
    row = pl.BlockSpec((tm, d), lambda i: (i, 0))
    vec = pl.BlockSpec((1, d), lambda i: (0, 0))
    kt = pl.BlockSpec((TOP_K, tm), lambda i: (0, i))
    return pl.pallas_call(
        _postmix_kernel,
        grid=(t // tm,),
        in_specs=[row, row,
                  pl.BlockSpec((1, N_MOD, d), lambda i: (i // tiles_per_seq, 0, 0)),
                  vec, vec,
                  pl.BlockSpec((n_exp, d), lambda i: (0, 0)),
                  pl.BlockSpec((n_exp, 1), lambda i: (0, 0))],
        out_specs=[row, pl.BlockSpec((tm, d // 2), lambda i: (i, 0)), kt, kt, kt,
                   pl.BlockSpec((n_exp, tm), lambda i: (0, i))],
        out_shape=[jax.ShapeDtypeStruct((t, d), F32),
                   jax.ShapeDtypeStruct((t, d // 2), U32),
                   jax.ShapeDtypeStruct((TOP_K, t), I32),
                   jax.ShapeDtypeStruct((TOP_K, t), F32),
                   jax.ShapeDtypeStruct((TOP_K, t), I32),
                   jax.ShapeDtypeStruct((n_exp, t), F32)],
        scratch_shapes=[pltpu.VMEM((n_exp, 1), F32)],
        compiler_params=_params("arbitrary"),
        name="post_mix_route",
    )(y, x2, mod, g_post.reshape(1, d), g_pre.reshape(1, d), w_router.T, router_bias.reshape(n_exp, 1))


def _dispatch_tables(idx_t, rank_t, cum, rows_per_block, tokens_per_tile):
    k, t = idx_t.shape
    n_exp = cum.shape[0]
    n = k * t
    n_blocks = n // rows_per_block + n_exp
    experts = jnp.arange(n_exp, dtype=I32)
    counts = cum[:, -1].astype(I32)
    padded = (counts + rows_per_block - 1) // rows_per_block * rows_per_block
    pad_ends = jnp.cumsum(padded)
    pad_starts = pad_ends - padded
    n_used = (pad_ends[-1] // rows_per_block).astype(I32)
    n_tab = n_blocks + EXPERT_LOOKAHEAD
    blk_start = jnp.arange(n_tab, dtype=I32) * rows_per_block
    block_e = jnp.minimum(jnp.sum((pad_ends[None, :] <= blk_start[:, None]).astype(I32), axis=1),
                          n_exp - 1).astype(I32)
    of_block = block_e[:, None] == experts[None, :]
    first_row = blk_start - jnp.sum(jnp.where(of_block, pad_starts[None, :], 0), axis=1)
    prev_e = jnp.concatenate([jnp.full((1,), -1, I32), block_e[:-1]])
    first = (block_e != prev_e).astype(I32)
    after = jnp.sum(jnp.where(of_block, pad_ends[None, :], 0), axis=1) // rows_per_block
    next_e = jnp.where(after < n_used, block_e[jnp.minimum(after, n_tab - 1)], -1).astype(I32)
    dest = jnp.sum(jnp.where(idx_t[:, :, None] == experts[None, None, :],
                             pad_starts[None, None, :], 0), axis=2) + rank_t
    n_tiles = t // tokens_per_tile
    dest_tiles = dest.reshape(k, n_tiles, tokens_per_tile).transpose(1, 0, 2).reshape(
        n_tiles, k * tokens_per_tile).astype(I32)
    return (block_e, first, next_e, first_row.astype(I32), n_used.reshape(1)), dest_tiles


def _expert_kernel(be_ref, first_ref, nexte_ref, row0_ref, nused_ref,
                   cum_hbm, ends_hbm, fpk_hbm, wg_hbm, wu_hbm, wd_hbm,
                   o_ref,
                   cum_v, ends_v, tok_v0, tok_v1, tok_s0, tok_s1, xs0, xs1,
                   stage_g, stage_u, stage_d, wg, wu, wd,
                   sem_cum, sem_tok, sem_x, sem_w, sem_fence, *, rows, n_tokens):
    i = pl.program_id(0)
    n_used = nused_ref[0]
    half = wd.shape[1] // 2
    n_buckets = cum_v.shape[1]
    tok_v, tok_s, xs = (tok_v0, tok_v1), (tok_s0, tok_s1), (xs0, xs1)

    def tok_copy(slot):
        return pltpu.make_async_copy(tok_v[slot], tok_s[slot], sem_tok.at[slot])

    def invert(blk, slot):
        e = be_ref[blk]
        row0 = row0_ref[blk]
        ends = ends_v[pl.ds(e, 1), :]
        counts = cum_v[e]
        c_hi = jnp.floor(counts * (1.0 / LANES))
        c_lo = counts - c_hi * LANES
        if n_buckets < LANES:
            fill = jnp.zeros((LANES - n_buckets, LANES), F32)
            c_hi = jnp.concatenate([c_hi, fill], axis=0)
            c_lo = jnp.concatenate([c_lo, fill], axis=0)
        c_hi = c_hi.astype(BF16)
        c_lo = c_lo.astype(BF16)
        ones = jnp.ones((SUBLANES, LANES), BF16)
        lane = lax.broadcasted_iota(I32, (LANES, LANES), 1)
        for c in range(rows // LANES):
            j = (row0 + c * LANES + lax.broadcasted_iota(I32, (LANES, 1), 0)).astype(F32)
            done = jnp.where(ends <= j, 1.0, 0.0)
            before = jnp.where(lane == 0, 1.0, pltpu.roll(done, 1, 1))
            pick = ((1.0 - done) * before).astype(BF16)
            in_bucket = (jnp.dot(pick, c_hi, preferred_element_type=F32) * LANES
                         + jnp.dot(pick, c_lo, preferred_element_type=F32))
            below = jnp.where(in_bucket <= j, 1.0, 0.0).astype(BF16)
            n_done = lax.dot_general(ones, done.astype(BF16), _NT, preferred_element_type=F32)
            n_below = lax.dot_general(ones, below, _NT, preferred_element_type=F32)
            tok = jnp.minimum(n_done * LANES + n_below, n_tokens - 1.0)
            tok_v[slot][:, c * LANES:(c + 1) * LANES] = tok.astype(I32)

    def gather_row(slot, r):
        tok = tok_s[slot][0, r]
        pltpu.make_async_copy(fpk_hbm.at[pl.ds(tok, 1), :], xs[slot].at[pl.ds(r, 1), :],
                              sem_x.at[slot]).start()

    def gather_start(slot, unrolled):
        if unrolled:
            for k in range(rows):
                gather_row(slot, (k * (rows // 2 + 1)) % rows)
        else:
            def body(r, c):
                gather_row(slot, r)
                return c
            lax.fori_loop(0, rows, body, 0, unroll=8)

    def gather_wait(slot):
        pltpu.make_async_copy(fpk_hbm.at[pl.ds(0, rows), :], xs[slot], sem_x.at[slot]).wait()

    def weight_pairs(e):
        return ((wg_hbm.at[e], stage_g), (wu_hbm.at[e], stage_u), (wd_hbm.at[e], stage_d))

    def weights_start(e):
        for k, (src, dst) in enumerate(weight_pairs(e)):
            pltpu.async_copy(src, dst, sem_w.at[k], priority=1)

    def weights_wait(e):
        for k, (src, dst) in enumerate(weight_pairs(e)):
            pltpu.make_async_copy(src, dst, sem_w.at[k]).wait()

    def cast_weights(src, dst):
        chunk = math.gcd(256, src.shape[0])
        def body(c, carry):
            r0 = pl.multiple_of(c * chunk, chunk)
            dst[pl.ds(r0, chunk), :] = src[pl.ds(r0, chunk), :].astype(BF16)
            return carry
        lax.fori_loop(0, src.shape[0] // chunk, body, 0)

    @pl.when((i == 0) & (n_used > 0))
    def _():
        cum_copy = pltpu.make_async_copy(cum_hbm, cum_v, sem_cum.at[0])
        ends_copy = pltpu.make_async_copy(ends_hbm, ends_v, sem_cum.at[1])
        cum_copy.start()
        ends_copy.start()
        weights_start(be_ref[0])
        cum_copy.wait()
        ends_copy.wait()
        invert(0, 0)
        tok_copy(0).start()
        tok_copy(0).wait()
        gather_start(0, unrolled=False)
        invert(1, 1)
        tok_copy(1).start()

    def run_block(cur):
        nxt = 1 - cur
        gather_wait(cur)
        tok_copy(nxt).wait()
        gather_start(nxt, unrolled=True)
        invert(i + 2, cur)
        lo, hi = _unpack_bf16_pair(xs[cur][...])
        x_lo = lo.astype(BF16)
        x_hi = hi.astype(BF16)
        g = (jnp.dot(x_lo, wg[0:half, :], preferred_element_type=F32)
             + jnp.dot(x_hi, wg[half:2 * half, :], preferred_element_type=F32))
        u = (jnp.dot(x_lo, wu[0:half, :], preferred_element_type=F32)
             + jnp.dot(x_hi, wu[half:2 * half, :], preferred_element_type=F32))
        act = (_silu(g) * u).astype(BF16)
        pl.semaphore_signal(sem_fence.at[0], 1)
        pl.semaphore_wait(sem_fence.at[0], 1)
        tok_copy(cur).start()
        o_ref[...] = _pack_bf16_pair(jnp.dot(act, wd[...], preferred_element_type=F32))

        @pl.when(i == n_used - 1)
        def _():
            gather_wait(nxt)
            tok_copy(cur).wait()

    @pl.when(i < n_used)
    def _():
        @pl.when(first_ref[i] == 1)
        def _():
            weights_wait(be_ref[i])
            cast_weights(stage_g, wg)
            cast_weights(stage_u, wu)
            cast_weights(stage_d, wd)

            @pl.when(nexte_ref[i] >= 0)
            def _():
                weights_start(nexte_ref[i])

        @pl.when(i % 2 == 0)
        def _():
            run_block(0)

        @pl.when(i % 2 == 1)
        def _():
            run_block(1)

    @pl.when(i >= n_used)
    def _():
        o_ref[...] = jnp.zeros(o_ref.shape, o_ref.dtype)


def _routed_experts(fpk, cum, block_tables, w_gate, w_up, w_down):
    block_e, first, next_e, first_row, n_used = block_tables
    t, half = fpk.shape
    d = 2 * half
    n_exp, _, de = w_gate.shape
    rows = MOE_ROWS
    n_blocks = block_e.shape[0] - EXPERT_LOOKAHEAD
    n_buckets = t // LANES
    assert n_buckets <= LANES, "two-level token search covers at most 128 buckets of 128 tokens"
    cum3 = cum.reshape(n_exp, n_buckets, LANES)
    ends = jnp.pad(cum3[:, :, LANES - 1], ((0, 0), (0, LANES - n_buckets)),
                   constant_values=float(2 * t))
    any_spec = pl.BlockSpec(memory_space=pl.ANY)
    grid_spec = pltpu.PrefetchScalarGridSpec(
        num_scalar_prefetch=5,
        grid=(n_blocks,),
        in_specs=[any_spec, any_spec, any_spec, any_spec, any_spec, any_spec],
        out_specs=pl.BlockSpec((rows, half), lambda i, *_: (i, 0)),
        scratch_shapes=[pltpu.VMEM((n_exp, n_buckets, LANES), F32),
                        pltpu.VMEM((n_exp, LANES), F32),
                        pltpu.VMEM((SUBLANES, rows), I32),
                        pltpu.VMEM((SUBLANES, rows), I32),
                        pltpu.SMEM((SUBLANES, rows), I32),
                        pltpu.SMEM((SUBLANES, rows), I32),
                        pltpu.VMEM((rows, half), U32),
                        pltpu.VMEM((rows, half), U32),
                        pltpu.VMEM((d, de), F32),
                        pltpu.VMEM((d, de), F32),
                        pltpu.VMEM((de, d), F32),
                        pltpu.VMEM((d, de), BF16),
                        pltpu.VMEM((d, de), BF16),
                        pltpu.VMEM((de, d), BF16),
                        pltpu.SemaphoreType.DMA((2,)),
                        pltpu.SemaphoreType.DMA((2,)),
                        pltpu.SemaphoreType.DMA((2,)),
                        pltpu.SemaphoreType.DMA((3,)),
                        pltpu.SemaphoreType.REGULAR((1,))])
    return pl.pallas_call(
        functools.partial(_expert_kernel, rows=rows, n_tokens=t),
        grid_spec=grid_spec,
        out_shape=jax.ShapeDtypeStruct((n_blocks * rows, half), U32),
        compiler_params=_params("arbitrary"),
        name="routed_experts",
    )(block_e, first, next_e, first_row, n_used, cum3, ends, fpk, w_gate, w_up, w_down)


def _shared_kernel(fpk_ref, wg_ref, wu_ref, wd_ref, o_ref):
    half = fpk_ref.shape[1]
    lo, hi = _unpack_bf16_pair(fpk_ref[...])
    x_lo = lo.astype(BF16)
    x_hi = hi.astype(BF16)
    g = (jnp.dot(x_lo, wg_ref[0:half, :], preferred_element_type=F32)
         + jnp.dot(x_hi, wg_ref[half:2 * half, :], preferred_element_type=F32))
    u = (jnp.dot(x_lo, wu_ref[0:half, :], preferred_element_type=F32)
         + jnp.dot(x_hi, wu_ref[half:2 * half, :], preferred_element_type=F32))
    act = (_silu(g) * u).astype(BF16)
    o_ref[...] = jnp.dot(act, wd_ref[...], preferred_element_type=F32).astype(o_ref.dtype)


def _shared_expert(fpk, wg_bf, wu_bf, wd_bf):
    t, half = fpk.shape
    d = 2 * half
    ds = wg_bf.shape[1]
    tm = min(512, t)
    return pl.pallas_call(
        _shared_kernel,
        grid=(t // tm,),
        in_specs=[pl.BlockSpec((tm, half), lambda i: (i, 0)),
                  pl.BlockSpec((d, ds), lambda i: (0, 0)),
                  pl.BlockSpec((d, ds), lambda i: (0, 0)),
                  pl.BlockSpec((ds, d), lambda i: (0, 0))],
        out_specs=pl.BlockSpec((tm, d), lambda i: (i, 0)),
        out_shape=jax.ShapeDtypeStruct((t, d), BF16),
        compiler_params=_params("arbitrary"),
        name="shared_expert",
    )(fpk, wg_bf, wu_bf, wd_bf)


def _combine_kernel(dest_hbm, ys_hbm, wts_ref, sh_ref, h_ref, mod_ref, g_ref, o_ref,
                    dest_s0, dest_s1, gbuf0, gbuf1, sem_d, sem_g, *, tm, top_k, n_tiles):
    i = pl.program_id(0)
    n_rows = top_k * tm
    half = gbuf0.shape[1]
    dest_s, gbuf = (dest_s0, dest_s1), (gbuf0, gbuf1)
    last = n_tiles - 1

    def dest_copy(tile, s):
        return pltpu.make_async_copy(dest_hbm.at[tile], dest_s[s], sem_d.at[s])

    def gather_row(s, r, priority):
        row = dest_s[s][r]
        pltpu.async_copy(ys_hbm.at[pl.ds(row, 1), :], gbuf[s].at[pl.ds(r, 1), :], sem_g.at[s],
                         priority=priority)

    def gather_start(s, unrolled):
        if unrolled:
            for r in range(n_rows):
                gather_row(s, r, r % 2)
        else:
            def body(r, c):
                gather_row(s, r, 0)
                return c
            lax.fori_loop(0, n_rows, body, 0, unroll=8)

    def gather_wait(s):
        pltpu.make_async_copy(ys_hbm.at[pl.ds(0, n_rows), :], gbuf[s], sem_g.at[s]).wait()

    @pl.when(i == 0)
    def _():
        first_dest = dest_copy(0, 0)
        first_dest.start()
        first_dest.wait()
        gather_start(0, unrolled=False)
        dest_copy(jnp.minimum(1, last), 1).start()

    def run_tile(cur):
        nxt = 1 - cur
        gather_wait(cur)
        dest_copy(jnp.minimum(i + 1, last), nxt).wait()
        gather_start(nxt, unrolled=True)
        dest_copy(jnp.minimum(i + 2, last), cur).start()
        acc_lo = jnp.zeros((tm, half), F32)
        acc_hi = jnp.zeros((tm, half), F32)
        for k in range(top_k):
            lo, hi = _unpack_bf16_pair(gbuf[cur][k * tm:(k + 1) * tm, :])
            w = wts_ref[:, k:k + 1]
            acc_lo = acc_lo + w * lo
            acc_hi = acc_hi + w * hi
        moe_lo = acc_lo + sh_ref[:, 0:half].astype(F32)
        moe_hi = acc_hi + sh_ref[:, half:2 * half].astype(F32)
        ssq = (jnp.sum(moe_lo * moe_lo, axis=1, keepdims=True)
               + jnp.sum(moe_hi * moe_hi, axis=1, keepdims=True))
        r = lax.rsqrt(ssq * (1.0 / (2 * half)) + NORM_EPS)
        gate2 = mod_ref[0, 5:6, :]
        g = g_ref[...]
        o_ref[:, 0:half] = h_ref[:, 0:half] + gate2[:, 0:half] * (moe_lo * r * g[:, 0:half])
        o_ref[:, half:2 * half] = (h_ref[:, half:2 * half]
                                   + gate2[:, half:2 * half] * (moe_hi * r * g[:, half:2 * half]))

        @pl.when(i == last)
        def _():
            gather_wait(nxt)
            dest_copy(last, cur).wait()

    @pl.when(i % 2 == 0)
    def _():
        run_tile(0)

    @pl.when(i % 2 == 1)
    def _():
        run_tile(1)


def _combine(ys, dest_tiles, wts, shared, h, mod, g_post, seq, top_k):
    t, d = h.shape
    tm = COMBINE_TOKENS
    n_tiles = t // tm
    tiles_per_seq = seq // tm
    row = pl.BlockSpec((tm, d), lambda i: (i, 0))
    any_spec = pl.BlockSpec(memory_space=pl.ANY)
    return pl.pallas_call(
        functools.partial(_combine_kernel, tm=tm, top_k=top_k, n_tiles=n_tiles),
        grid=(n_tiles,),
        in_specs=[any_spec, any_spec,
                  pl.BlockSpec((tm, top_k), lambda i: (i, 0)),
                  row, row,
                  pl.BlockSpec((1, N_MOD, d), lambda i: (i // tiles_per_seq, 0, 0)),
                  pl.BlockSpec((1, d), lambda i: (0, 0))],
        out_specs=row,
        out_shape=jax.ShapeDtypeStruct((t, d), F32),
        scratch_shapes=[pltpu.SMEM((top_k * tm,), I32),
                        pltpu.SMEM((top_k * tm,), I32),
                        pltpu.VMEM((top_k * tm, d // 2), U32),
                        pltpu.VMEM((top_k * tm, d // 2), U32),
                        pltpu.SemaphoreType.DMA((2,)),
                        pltpu.SemaphoreType.DMA((2,))],
        compiler_params=_params("arbitrary"),
        name="combine",
    )(dest_tiles, ys, wts, shared, h, mod, g_post.reshape(1, d))


def _rope_tables(seq):
    half = HEAD_DIM // 2
    inv_freq = jnp.exp(-math.log(ROPE_THETA) * jnp.arange(half, dtype=F32) / half)
    ang = jnp.arange(seq, dtype=F32)[:, None] * inv_freq[None, :]
    cos, sin = jnp.cos(ang), jnp.sin(ang)
    return jnp.concatenate([cos, cos], axis=1), jnp.concatenate([-sin, sin], axis=1)


def kernel(x, c, w_mod, b_mod, g_pre_mix, g_post_mix, g_pre_ffn, g_post_ffn, w_in, w_pool, pool_scale, g_attn_out, g_pool_out, w_out, w_router, router_bias, w_gate, w_up, w_down, w_shared_gate, w_shared_up, w_shared_down):
    batch, seq, d = x.shape
    depth = w_mod.shape[0]
    n_groups, gd = w_pool.shape[1], w_pool.shape[2]
    pool_width = n_groups * gd
    attn_width = (w_in.shape[2] - pool_width) // 3
    n_heads = attn_width // HEAD_DIM
    assert n_groups == len(POOL_WINDOWS) and seq % (ATTN_BLOCK * DILATIONS[-1]) == 0
    cosf, sinf = _rope_tables(seq)

    h = x.reshape(batch * seq, d)
    for layer in range(depth):
        mod = _modulation(c, w_mod[layer], b_mod[layer])
        proj = _in_projection(h, g_pre_mix[layer], mod, w_in[layer].astype(BF16), cosf, sinf,
                              seq, attn_width)
        attn = _dilated_attention(proj, batch, seq, n_heads)
        mixed = _mixer_norms(attn, proj, w_pool[layer].astype(BF16), pool_scale[layer],
                             g_attn_out[layer], g_pool_out[layer], batch, seq)
        y = _matmul(mixed, w_out[layer].astype(BF16), F32)
        h, fpk, idx_t, wts_t, rank_t, cum = _post_mix_route(
            y, h, mod, g_post_mix[layer], g_pre_ffn[layer], w_router[layer], router_bias[layer], seq)
        block_tables, dest_tiles = _dispatch_tables(idx_t, rank_t, cum, MOE_ROWS, COMBINE_TOKENS)
        ys = _routed_experts(fpk, cum, block_tables, w_gate[layer], w_up[layer], w_down[layer])
        shared = _shared_expert(fpk, w_shared_gate[layer].astype(BF16),
                                w_shared_up[layer].astype(BF16), w_shared_down[layer].astype(BF16))
        h = _combine(ys, dest_tiles, wts_t.T, shared, h, mod, g_post_ffn[layer], seq, TOP_K)
    return h.reshape(batch, seq, d)
```
